```python
import jax, jax.numpy as jnp
from jax import lax
import numpy as np

D_MODEL = 1024
BATCH = 32
SEQ = 2048
DEPTH = 1

MLA_HEADS = 4
QK_NOPE_DIM = 128
QK_ROPE_DIM = 64
QK_DIM = QK_NOPE_DIM + QK_ROPE_DIM
V_HEAD_DIM = 128
Q_LORA_RANK = 512
KV_LORA_RANK = 256
MLA_WIDTH = MLA_HEADS * V_HEAD_DIM
ROPE_THETA = 10000.0
Q_BLOCK = 128
POOL_WINDOWS = (2, 4, 8, 16)
POOL_GROUPS = len(POOL_WINDOWS)
POOL_WIDTH = D_MODEL - MLA_WIDTH
POOL_GROUP_DIM = POOL_WIDTH // POOL_GROUPS
MIX_WIDTH = MLA_WIDTH + POOL_WIDTH
IN_SPLITS = (Q_LORA_RANK, KV_LORA_RANK, QK_ROPE_DIM, MLA_WIDTH, POOL_WIDTH, POOL_WIDTH)
IN_WIDTH = sum(IN_SPLITS)
RMS_EPS = 1e-6
LN_EPS = 1e-5

kernel_name = "hybrid_mla_multiscale_pool_deepnorm"


def rms_norm(x, g):
    xf = x.astype(jnp.float32)
    inv = lax.rsqrt(jnp.mean(xf * xf, axis=-1, keepdims=True) + RMS_EPS)
    return (xf * inv).astype(x.dtype) * g


def layer_norm(x, g, b):
    xf = x.astype(jnp.float32)
    mu = jnp.mean(xf, axis=-1, keepdims=True)
    var = jnp.mean(jnp.square(xf - mu), axis=-1, keepdims=True)
    return ((xf - mu) * lax.rsqrt(var + LN_EPS)).astype(x.dtype) * g + b


def rope_cos_sin(positions, dtype):
    half = QK_ROPE_DIM // 2
    inv_freq = ROPE_THETA ** (-jnp.arange(half, dtype=jnp.float32) / half)
    ang = positions.astype(jnp.float32)[..., None] * inv_freq
    return jnp.cos(ang).astype(dtype), jnp.sin(ang).astype(dtype)


def apply_rope(t, cos, sin):
    t1, t2 = jnp.split(t, 2, axis=-1)
    return jnp.concatenate([t1 * cos - t2 * sin, t1 * sin + t2 * cos], axis=-1)


def mla_branch(x_q, x_kv, k_rope_raw, positions, q_norm_g, w_uq, kv_norm_g, w_ukv):
    B, S, _ = x_q.shape
    cos, sin = rope_cos_sin(positions, x_q.dtype)
    q = (rms_norm(x_q, q_norm_g) @ w_uq).reshape(B, S, MLA_HEADS, QK_DIM)
    q_nope, q_rope = q[..., :QK_NOPE_DIM], q[..., QK_NOPE_DIM:]
    q_rope = apply_rope(q_rope, cos[:, :, None, :], sin[:, :, None, :])
    kv = (rms_norm(x_kv, kv_norm_g) @ w_ukv).reshape(B, S, MLA_HEADS, QK_NOPE_DIM + V_HEAD_DIM)
    k_nope, v = kv[..., :QK_NOPE_DIM], kv[..., QK_NOPE_DIM:]
    k_rope = apply_rope(k_rope_raw, cos, sin)
    scale = QK_DIM ** -0.5
    nb = S // Q_BLOCK
    qn_b = q_nope.reshape(B, nb, Q_BLOCK, MLA_HEADS, QK_NOPE_DIM).transpose(1, 0, 2, 3, 4)
    qr_b = q_rope.reshape(B, nb, Q_BLOCK, MLA_HEADS, QK_ROPE_DIM).transpose(1, 0, 2, 3, 4)
    pos_b = positions.reshape(B, nb, Q_BLOCK).transpose(1, 0, 2)
    neg = jnp.finfo(jnp.float32).min

    def attend(args):
        qn, qr, pq = args
        s = (jnp.einsum('bqhd,bkhd->bhqk', qn, k_nope)
             + jnp.einsum('bqhr,bkr->bhqk', qr, k_rope)).astype(jnp.float32) * scale
        mask = pq[:, None, :, None] >= positions[:, None, None, :]
        p = jax.nn.softmax(jnp.where(mask, s, neg), axis=-1)
        return jnp.einsum('bhqk,bkhd->bqhd', p.astype(v.dtype), v)

    o = lax.map(attend, (qn_b, qr_b, pos_b))
    return o.transpose(1, 0, 2, 3, 4).reshape(B, S, MLA_WIDTH)


def pool_branch(u, pool_w, pool_scale):
    B, S, _ = u.shape
    uf = u.astype(jnp.float32).reshape(B, S, POOL_GROUPS, POOL_GROUP_DIM)
    cs = jnp.concatenate([jnp.zeros((B, 1, POOL_GROUPS, POOL_GROUP_DIM), jnp.float32),
                          jnp.cumsum(uf, axis=1)], axis=1)
    hi = jnp.arange(S) + 1
    means = []
    for g, w in enumerate(POOL_WINDOWS):
        lo = jnp.maximum(hi - w, 0)
        cnt = (hi - lo).astype(jnp.float32)[None, :, None]
        means.append((cs[:, hi, g] - cs[:, lo, g]) / cnt)
    pooled = jnp.stack(means, axis=2) - uf
    mixed = jnp.einsum('bsgc,gcd->bsgd', pooled.astype(u.dtype), pool_w)
    return mixed.reshape(B, S, POOL_WIDTH) * pool_scale


def setup_inputs(seed: int = 0) -> dict:
    key = jax.random.key(seed)
    ks = jax.random.split(key, 12)
    beta = (8.0 * DEPTH) ** -0.25
    nrm = jax.random.normal
    return {
        "x": nrm(ks[0], (BATCH, SEQ, D_MODEL), jnp.float32),
        "positions": jnp.broadcast_to(jnp.arange(SEQ, dtype=jnp.int32), (BATCH, SEQ)),
        "w_in": nrm(ks[1], (D_MODEL, IN_WIDTH), jnp.float32) * D_MODEL ** -0.5,
        "q_norm_g": 1.0 + 0.05 * nrm(ks[2], (Q_LORA_RANK,), jnp.float32),
        "w_uq": nrm(ks[3], (Q_LORA_RANK, MLA_HEADS * QK_DIM), jnp.float32) * Q_LORA_RANK ** -0.5,
        "kv_norm_g": 1.0 + 0.05 * nrm(ks[4], (KV_LORA_RANK,), jnp.float32),
        "w_ukv": nrm(ks[5], (KV_LORA_RANK, MLA_HEADS * (QK_NOPE_DIM + V_HEAD_DIM)), jnp.float32) * KV_LORA_RANK ** -0.5,
        "pool_w": nrm(ks[6], (POOL_GROUPS, POOL_GROUP_DIM, POOL_GROUP_DIM), jnp.float32) * POOL_GROUP_DIM ** -0.5,
        "pool_scale": 1.0 + 0.1 * nrm(ks[7], (POOL_WIDTH,), jnp.float32),
        "w_out": nrm(ks[8], (MIX_WIDTH, D_MODEL), jnp.float32) * (MIX_WIDTH ** -0.5) * beta,
        "ln_g": 1.0 + 0.05 * nrm(ks[9], (DEPTH, D_MODEL), jnp.float32),
        "ln_b": 0.02 * nrm(ks[10], (DEPTH, D_MODEL), jnp.float32),
    }


def reference(x, positions, w_in, q_norm_g, w_uq, kv_norm_g, w_ukv, pool_w, pool_scale, w_out, ln_g, ln_b):
    alpha = (2.0 * DEPTH) ** 0.25
    splits = [int(c) for c in np.cumsum(IN_SPLITS)[:-1]]
    for layer in range(DEPTH):
        h = x @ w_in
        x_q, x_kv, k_rope_raw, gate_a, u_pool, gate_b = jnp.split(h, splits, axis=-1)
        y_a = mla_branch(x_q, x_kv, k_rope_raw, positions, q_norm_g, w_uq, kv_norm_g, w_ukv) * jax.nn.silu(gate_a)
        y_b = pool_branch(u_pool, pool_w, pool_scale) * jax.nn.silu(gate_b)
        mix = jnp.concatenate([y_a, y_b], axis=-1) @ w_out
        x = layer_norm(alpha * x + mix, ln_g[layer], ln_b[layer])
    return x
```

```python
import functools
import math

import jax
import jax.numpy as jnp
import numpy as np
from jax import lax
from jax.experimental import pallas as pl
from jax.experimental.pallas import tpu as pltpu

D_MODEL = 1024
DEPTH = 1
HEADS = 4
NOPE = 128
ROPE = 64
QK_DIM = NOPE + ROPE
V_DIM = 128
Q_RANK = 512
KV_RANK = 256
MLA_WIDTH = HEADS * V_DIM
POOL_WINDOWS = (2, 4, 8, 16)
POOL_GROUP = 128
POOL_WIDTH = len(POOL_WINDOWS) * POOL_GROUP
ROPE_THETA = 10000.0
RMS_EPS = 1e-6
LN_EPS = 1e-5

LANES = 128
MAX_WINDOW = max(POOL_WINDOWS)
VMEM_LIMIT_BYTES = 56 * 1024 * 1024

_C_Q = 0
_C_KV = _C_Q + Q_RANK
_C_KR = _C_KV + KV_RANK
_C_GA = _C_KR + 2 * ROPE
_C_U = _C_GA + MLA_WIDTH
_C_GB = _C_U + POOL_WIDTH
_C_END = _C_GB + POOL_WIDTH

_SCORE_SCALE = (QK_DIM ** -0.5) * math.log2(math.e)
_MASKED = -1e30

TOKEN_TILE = 512
Q_TILE = 512
K_TILE = 512

_bf16 = jnp.bfloat16
_f32 = jnp.float32


def _dot(a, b):
    return jnp.dot(a, b, preferred_element_type=_f32)


def _rms_norm(h, g):
    ms = jnp.mean(h * h, axis=-1, keepdims=True)
    return (h * lax.rsqrt(ms + RMS_EPS)) * g


def _silu(z):
    return z * (1.0 / (1.0 + jnp.exp(-z)))


def _proj_kernel(x_ref, pos_ref, freq_ref, w1_ref, qg_ref, wq_ref, kvg_ref, wkv_ref,
                 wp_ref, ps_ref,
                 qn_ref, qr_ref, kn_ref, kd_ref, v_ref, ga_ref, yb_ref,
                 carry_ref):
    tm = x_ref.shape[1]
    si = pl.program_id(1)
    xb = x_ref[0].astype(_bf16)

    ang = pos_ref[0].astype(_f32) * freq_ref[...]
    lane = lax.broadcasted_iota(jnp.int32, ang.shape, 1)
    tab = jnp.where(lane < ROPE, jnp.cos(ang), jnp.sin(ang))

    xqn = _rms_norm(_dot(xb, w1_ref[:, _C_Q:_C_KV]), qg_ref[...])
    q = _dot(xqn.astype(_bf16), wq_ref[...])
    qn_ref[0] = (q[:, :HEADS * NOPE] * _SCORE_SCALE).astype(_bf16)
    qtab = tab * _SCORE_SCALE
    for h in range(HEADS):
        lo = HEADS * NOPE + h * LANES
        qr_ref[0, :, h * LANES:(h + 1) * LANES] = (q[:, lo:lo + LANES] * qtab).astype(_bf16)

    xkvn = _rms_norm(_dot(xb, w1_ref[:, _C_KV:_C_KR]), kvg_ref[...])
    kv = _dot(xkvn.astype(_bf16), wkv_ref[...])
    kn_ref[0] = kv[:, :HEADS * NOPE].astype(_bf16)
    v_ref[0] = kv[:, HEADS * NOPE:].astype(_bf16)
    t = _dot(xb, w1_ref[:, _C_KR:_C_GA]) * tab
    kd_ref[0] = (t + pltpu.roll(t, ROPE, axis=1)).astype(_bf16)

    ga_ref[0] = _silu(_dot(xb, w1_ref[:, _C_GA:_C_U])).astype(_bf16)

    u = _dot(xb, w1_ref[:, _C_U:_C_GB])

    @pl.when(si == 0)
    def _():
        carry_ref[...] = jnp.zeros_like(carry_ref)

    ext = jnp.concatenate([carry_ref[...], u], axis=0)
    carry_ref[...] = u[tm - MAX_WINDOW:, :]
    sums = []
    acc = ext
    width = 1
    for g, w in enumerate(POOL_WINDOWS):
        while width < w:
            acc = acc + pltpu.roll(acc, width, axis=0)
            width *= 2
        sums.append(acc[MAX_WINDOW:, :POOL_GROUP])
        acc = acc[:, POOL_GROUP:]
    tok = si * tm + lax.broadcasted_iota(jnp.int32, (tm, 1), 0)
    pooled = []
    for g, w in enumerate(POOL_WINDOWS):
        inv_cnt = 1.0 / jnp.minimum(tok + 1, w).astype(_f32)
        pooled.append(sums[g] * inv_cnt - u[:, g * POOL_GROUP:(g + 1) * POOL_GROUP])
    mixed = jnp.concatenate(
        [_dot(jnp.concatenate(pooled[0:2], axis=1).astype(_bf16), wp_ref[0]),
         _dot(jnp.concatenate(pooled[2:4], axis=1).astype(_bf16), wp_ref[1])], axis=1)
    gb = _silu(_dot(xb, w1_ref[:, _C_GB:_C_END]))
    yb_ref[0] = (mixed * ps_ref[...] * gb).astype(_bf16)


def _attn_kernel(need_ref, full_ref,
                 qn_ref, qr_ref, kn_ref, kd_ref, v_ref, ga_ref, pq_ref, pk_ref,
                 o_ref,
                 m_ref, l_ref, acc_ref, *, nq, nk):
    tq = qn_ref.shape[1]
    tk = K_TILE
    b = pl.program_id(0)
    qi = pl.program_id(1)

    m_ref[...] = jnp.full(m_ref.shape, _MASKED, _f32)
    l_ref[...] = jnp.zeros_like(l_ref)
    acc_ref[...] = jnp.zeros_like(acc_ref)

    def tile(kj, masked):
        ks = pl.multiple_of(kj * tk, tk)
        kd = kd_ref[0, pl.ds(ks, tk), :]
        if masked:
            keep = pq_ref[0] >= pk_ref[0, :, pl.ds(ks, tk)]
        for h in range(HEADS):
            cols = slice(h * LANES, (h + 1) * LANES)
            qc = jnp.concatenate([qn_ref[0, :, cols], qr_ref[0, :, cols]], axis=1)
            kc = jnp.concatenate([kn_ref[0, pl.ds(ks, tk), cols], kd], axis=1)
            s = lax.dot_general(qc, kc, (((1,), (1,)), ((), ())),
                                preferred_element_type=_f32)
            if masked:
                s = jnp.where(keep, s, _MASKED)
            m_old = m_ref[h]
            m_new = jnp.maximum(m_old, jnp.max(s, axis=-1, keepdims=True))
            alpha = jnp.exp2(m_old - m_new)
            p = jnp.exp2(s - m_new)
            l_ref[h] = alpha * l_ref[h] + jnp.sum(p, axis=-1, keepdims=True)
            acc_ref[h] = alpha * acc_ref[h] + _dot(p.astype(_bf16),
                                                   v_ref[0, pl.ds(ks, tk), cols])
            m_ref[h] = m_new

    def kv_step(kj, carry):
        idx = (b * nq + qi) * nk + kj

        @pl.when(jnp.logical_and(need_ref[idx] != 0, full_ref[idx] != 0))
        def _():
            tile(kj, masked=False)

        @pl.when(jnp.logical_and(need_ref[idx] != 0, full_ref[idx] == 0))
        def _():
            tile(kj, masked=True)

        return carry

    lax.fori_loop(0, nk, kv_step, 0)

    for h in range(HEADS):
        cols = slice(h * LANES, (h + 1) * LANES)
        o = acc_ref[h] * (1.0 / l_ref[h])
        o_ref[0, :, cols] = (o * ga_ref[0, :, cols].astype(_f32)).astype(_bf16)


def _out_kernel(x_ref, ya_ref, yb_ref, wo_ref, g_ref, b_ref, o_ref, *, alpha):
    y = jnp.concatenate([ya_ref[0], yb_ref[0]], axis=1)
    z = alpha * x_ref[0] + _dot(y, wo_ref[...])
    mu = jnp.mean(z, axis=-1, keepdims=True)
    zc = z - mu
    var = jnp.mean(zc * zc, axis=-1, keepdims=True)
    o_ref[0] = zc * lax.rsqrt(var + LN_EPS) * g_ref[...] + b_ref[...]


def _rotate_half_cols(w):
    half = w.shape[-1] // 2
    return jnp.concatenate([-w[..., half:], w[..., :half]], axis=-1)


def _prepare_weights(w_in, w_uq, w_ukv, pool_w):
    splits = np.cumsum([Q_RANK, KV_RANK, ROPE, MLA_WIDTH, POOL_WIDTH, POOL_WIDTH])[:-1]
    wq_l, wkv_l, wkr, wga, wu, wgb = jnp.split(w_in, [int(c) for c in splits], axis=1)
    w1 = jnp.concatenate([wq_l, wkv_l, wkr, _rotate_half_cols(wkr), wga, wu, wgb], axis=1)

    wq3 = w_uq.reshape(Q_RANK, HEADS, QK_DIM)
    wq_nope = wq3[:, :, :NOPE].reshape(Q_RANK, HEADS * NOPE)
    wq_rope = wq3[:, :, NOPE:]
    wq_pair = jnp.concatenate([wq_rope, _rotate_half_cols(wq_rope)], axis=-1)
    wq = jnp.concatenate([wq_nope, wq_pair.reshape(Q_RANK, HEADS * 2 * ROPE)], axis=1)

    wkv3 = w_ukv.reshape(KV_RANK, HEADS, NOPE + V_DIM)
    wkv = jnp.concatenate([wkv3[:, :, :NOPE].reshape(KV_RANK, HEADS * NOPE),
                           wkv3[:, :, NOPE:].reshape(KV_RANK, HEADS * V_DIM)], axis=1)

    z = jnp.zeros((POOL_GROUP, POOL_GROUP), pool_w.dtype)
    wp = jnp.stack([jnp.block([[pool_w[0], z], [z, pool_w[1]]]),
                    jnp.block([[pool_w[2], z], [z, pool_w[3]]])])
    return w1.astype(_bf16), wq.astype(_bf16), wkv.astype(_bf16), wp.astype(_bf16)


def _const_spec(shape):
    return pl.BlockSpec(shape, lambda *_: (0,) * len(shape))


def _layer(x, positions, w1, q_norm_g, wq, kv_norm_g, wkv, wp, pool_scale, w_out, ln_g, ln_b):
    B, S, D = x.shape
    tm, tq, tk = TOKEN_TILE, Q_TILE, K_TILE
    ns, nq, nk = S // tm, S // tq, S // tk

    half = ROPE // 2
    inv_freq = ROPE_THETA ** (-jnp.arange(half, dtype=_f32) / half)
    freq_row = jnp.tile(inv_freq, LANES // half).reshape(1, LANES)
    pos_col = positions.reshape(B, S, 1)
    pos_row = positions.reshape(B, 1, S)

    def tok_spec(width):
        return pl.BlockSpec((1, tm, width), lambda b, s: (b, s, 0))

    act = lambda width: jax.ShapeDtypeStruct((B, S, width), _bf16)
    qn, qr, kn, kd, v, ga, yb = pl.pallas_call(
        _proj_kernel,
        grid=(B, ns),
        in_specs=[tok_spec(D), tok_spec(1), _const_spec((1, LANES)),
                  _const_spec(w1.shape), _const_spec((1, Q_RANK)), _const_spec(wq.shape),
                  _const_spec((1, KV_RANK)), _const_spec(wkv.shape),
                  _const_spec(wp.shape), _const_spec((1, POOL_WIDTH))],
        out_specs=[tok_spec(MLA_WIDTH), tok_spec(MLA_WIDTH), tok_spec(MLA_WIDTH),
                   tok_spec(LANES), tok_spec(MLA_WIDTH), tok_spec(MLA_WIDTH),
                   tok_spec(POOL_WIDTH)],
        out_shape=[act(MLA_WIDTH), act(MLA_WIDTH), act(MLA_WIDTH), act(LANES),
                   act(MLA_WIDTH), act(MLA_WIDTH), act(POOL_WIDTH)],
        scratch_shapes=[pltpu.VMEM((MAX_WINDOW, POOL_WIDTH), _f32)],
        compiler_params=pltpu.CompilerParams(
            dimension_semantics=("arbitrary", "arbitrary"),
            vmem_limit_bytes=VMEM_LIMIT_BYTES),
        name="mla_pool_proj",
    )(x, pos_col, freq_row, w1, q_norm_g.reshape(1, -1), wq, kv_norm_g.reshape(1, -1), wkv,
      wp, pool_scale.reshape(1, -1))

    pq_blk = positions.reshape(B, nq, tq)
    pk_blk = positions.reshape(B, nk, tk)
    need = pq_blk.max(-1)[:, :, None] >= pk_blk.min(-1)[:, None, :]
    full = pq_blk.min(-1)[:, :, None] >= pk_blk.max(-1)[:, None, :]

    q_spec = pl.BlockSpec((1, tq, MLA_WIDTH), lambda b, i, *_: (b, i, 0))
    kv_spec = lambda width: pl.BlockSpec((1, S, width), lambda b, i, *_: (b, 0, 0))
    ya = pl.pallas_call(
        functools.partial(_attn_kernel, nq=nq, nk=nk),
        grid_spec=pltpu.PrefetchScalarGridSpec(
            num_scalar_prefetch=2,
            grid=(B, nq),
            in_specs=[q_spec, q_spec, kv_spec(MLA_WIDTH), kv_spec(LANES), kv_spec(MLA_WIDTH),
                      q_spec,
                      pl.BlockSpec((1, tq, 1), lambda b, i, *_: (b, i, 0)),
                      pl.BlockSpec((1, 1, S), lambda b, i, *_: (b, 0, 0))],
            out_specs=q_spec,
            scratch_shapes=[pltpu.VMEM((HEADS, tq, 1), _f32),
                            pltpu.VMEM((HEADS, tq, 1), _f32),
                            pltpu.VMEM((HEADS, tq, V_DIM), _f32)]),
        out_shape=act(MLA_WIDTH),
        compiler_params=pltpu.CompilerParams(
            dimension_semantics=("arbitrary", "arbitrary"),
            vmem_limit_bytes=VMEM_LIMIT_BYTES),
        name="mla_attention",
    )(need.reshape(-1).astype(jnp.int32), full.reshape(-1).astype(jnp.int32),
      qn, qr, kn, kd, v, ga, pos_col, pos_row)

    alpha = (2.0 * DEPTH) ** 0.25
    return pl.pallas_call(
        functools.partial(_out_kernel, alpha=alpha),
        grid=(B, ns),
        in_specs=[tok_spec(D), tok_spec(MLA_WIDTH), tok_spec(POOL_WIDTH),
                  _const_spec(w_out.shape), _const_spec((1, D)), _const_spec((1, D))],
        out_specs=tok_spec(D),
        out_shape=jax.ShapeDtypeStruct((B, S, D), x.dtype),
        compiler_params=pltpu.CompilerParams(
            dimension_semantics=("arbitrary", "arbitrary"),
            vmem_limit_bytes=VMEM_LIMIT_BYTES),
        name="out_proj_layernorm",
    )(x, ya, yb, w_out, ln_g.reshape(1, -1), ln_b.reshape(1, -1))


def kernel(x, positions, w_in, q_norm_g, w_uq, kv_norm_g, w_ukv, pool_w, pool_scale, w_out,
           ln_g, ln_b):
    w1, wq, wkv, wp = _prepare_weights(w_in, w_uq, w_ukv, pool_w)
    w_out_b = w_out.astype(_bf16)
    for layer in range(DEPTH):
        x = _layer(x, positions, w1, q_norm_g, wq, kv_norm_g, wkv, wp, pool_scale, w_out_b,
                   ln_g[layer], ln_b[layer])
    return x
```

```python
import functools
import math

import jax
import jax.numpy as jnp
import numpy as np
from jax import lax
from jax.experimental import pallas as pl
from jax.experimental.pallas import tpu as pltpu

D_MODEL = 1024
DEPTH = 1
HEADS = 4
NOPE = 128
ROPE = 64
QK_DIM = NOPE + ROPE
V_DIM = 128
Q_RANK = 512
KV_RANK = 256
MLA_WIDTH = HEADS * V_DIM
POOL_WINDOWS = (2, 4, 8, 16)
POOL_GROUP = 128
POOL_WIDTH = len(POOL_WINDOWS) * POOL_GROUP
ROPE_THETA = 10000.0
RMS_EPS = 1e-6
LN_EPS = 1e-5

LANES = 128
MAX_WINDOW = max(POOL_WINDOWS)
VMEM_LIMIT_BYTES = 56 * 1024 * 1024

_C_Q = 0
_C_KV = _C_Q + Q_RANK
_C_KR = _C_KV + KV_RANK
_C_GA = _C_KR + 2 * ROPE
_C_U = _C_GA + MLA_WIDTH
_C_GB = _C_U + POOL_WIDTH
_C_END = _C_GB + POOL_WIDTH

_SCORE_SCALE = (QK_DIM ** -0.5) * math.log2(math.e)
_MASKED = -1e30

TOKEN_TILE = 512
Q_TILE = 512
K_TILE = 512

_bf16 = jnp.bfloat16
_f32 = jnp.float32


def _dot(a, b):
    return jnp.dot(a, b, preferred_element_type=_f32)


def _rms_norm(h, g):
    ms = jnp.mean(h * h, axis=-1, keepdims=True)
    return (h * lax.rsqrt(ms + RMS_EPS)) * g


def _silu(z):
    return z * (1.0 / (1.0 + jnp.exp(-z)))


def _dot_nt(a, b):
    return lax.dot_general(a, b, (((1,), (1,)), ((), ())), preferred_element_type=_f32)


def _proj_kernel(x_ref, pos_ref, freq_ref, w1_ref, qg_ref, wqt_ref, kvg_ref, wkn_ref, wvt_ref,
                 wp_ref, ps_ref,
                 qnt_ref, qrt_ref, kn_ref, kd_ref, vt_ref, ga_ref, yb_ref,
                 carry_ref):
    tm = x_ref.shape[1]
    si = pl.program_id(1)
    xb = x_ref[0].astype(_bf16)

    ang = pos_ref[0].astype(_f32) * freq_ref[...]
    lane = lax.broadcasted_iota(jnp.int32, ang.shape, 1)
    tab = jnp.where(lane < ROPE, jnp.cos(ang), jnp.sin(ang))
    qtab_t = tab.T * _SCORE_SCALE

    xqn = _rms_norm(_dot(xb, w1_ref[:, _C_Q:_C_KV]), qg_ref[...])
    qt = _dot_nt(wqt_ref[...], xqn.astype(_bf16))
    qnt_ref[0] = (qt[:HEADS * NOPE] * _SCORE_SCALE).astype(_bf16)
    for h in range(HEADS):
        lo = HEADS * NOPE + h * LANES
        qrt_ref[0, h * LANES:(h + 1) * LANES, :] = (qt[lo:lo + LANES] * qtab_t).astype(_bf16)

    xkvn = _rms_norm(_dot(xb, w1_ref[:, _C_KV:_C_KR]), kvg_ref[...]).astype(_bf16)
    kn_ref[0] = _dot(xkvn, wkn_ref[...]).astype(_bf16)
    vt_ref[0] = _dot_nt(wvt_ref[...], xkvn).astype(_bf16)
    t = _dot(xb, w1_ref[:, _C_KR:_C_GA]) * tab
    kd_ref[0] = (t + pltpu.roll(t, ROPE, axis=1)).astype(_bf16)

    ga_ref[0] = _silu(_dot(xb, w1_ref[:, _C_GA:_C_U])).astype(_bf16)

    u = _dot(xb, w1_ref[:, _C_U:_C_GB])

    @pl.when(si == 0)
    def _():
        carry_ref[...] = jnp.zeros_like(carry_ref)

    ext = jnp.concatenate([carry_ref[...], u], axis=0)
    carry_ref[...] = u[tm - MAX_WINDOW:, :]
    sums = []
    acc = ext
    width = 1
    for g, w in enumerate(POOL_WINDOWS):
        while width < w:
            acc = acc + pltpu.roll(acc, width, axis=0)
            width *= 2
        sums.append(acc[MAX_WINDOW:, :POOL_GROUP])
        acc = acc[:, POOL_GROUP:]
    tok = si * tm + lax.broadcasted_iota(jnp.int32, (tm, 1), 0)
    pooled = []
    for g, w in enumerate(POOL_WINDOWS):
        inv_cnt = 1.0 / jnp.minimum(tok + 1, w).astype(_f32)
        pooled.append(sums[g] * inv_cnt - u[:, g * POOL_GROUP:(g + 1) * POOL_GROUP])
    mixed = jnp.concatenate(
        [_dot(jnp.concatenate(pooled[0:2], axis=1).astype(_bf16), wp_ref[0]),
         _dot(jnp.concatenate(pooled[2:4], axis=1).astype(_bf16), wp_ref[1])], axis=1)
    gb = _silu(_dot(xb, w1_ref[:, _C_GB:_C_END]))
    yb_ref[0] = (mixed * ps_ref[...] * gb).astype(_bf16)


def _attn_kernel(need_ref, full_ref,
                 qnt_ref, qrt_ref, kn_ref, kd_ref, vt_ref, ga_ref, pq_ref, pk_ref,
                 o_ref,
                 m_ref, l_ref, acc_ref, *, nq, nk):
    tk = K_TILE
    b = pl.program_id(0)
    qi = pl.program_id(1)

    m_ref[...] = jnp.full(m_ref.shape, _MASKED, _f32)
    l_ref[...] = jnp.zeros_like(l_ref)
    acc_ref[...] = jnp.zeros_like(acc_ref)

    def tile(kj, masked):
        ks = pl.multiple_of(kj * tk, tk)
        kd = kd_ref[0, pl.ds(ks, tk), :]
        if masked:
            keep = pk_ref[0, pl.ds(ks, tk), :] <= pq_ref[0]
        for h in range(HEADS):
            rows = slice(h * LANES, (h + 1) * LANES)
            qc = jnp.concatenate([qnt_ref[0, rows, :], qrt_ref[0, rows, :]], axis=0)
            kc = jnp.concatenate([kn_ref[0, pl.ds(ks, tk), rows], kd], axis=1)
            s = _dot(kc, qc)
            if masked:
                s = jnp.where(keep, s, _MASKED)
            m_old = m_ref[h]
            m_new = jnp.maximum(m_old, jnp.max(s, axis=0, keepdims=True))
            alpha = jnp.exp2(m_old - m_new)
            p = jnp.exp2(s - m_new)
            l_ref[h] = alpha * l_ref[h] + jnp.sum(p, axis=0, keepdims=True)
            acc_ref[h] = alpha * acc_ref[h] + _dot(vt_ref[0, rows, pl.ds(ks, tk)],
                                                   p.astype(_bf16))
            m_ref[h] = m_new

    def kv_step(kj, carry):
        idx = (b * nq + qi) * nk + kj

        @pl.when(jnp.logical_and(need_ref[idx] != 0, full_ref[idx] != 0))
        def _():
            tile(kj, masked=False)

        @pl.when(jnp.logical_and(need_ref[idx] != 0, full_ref[idx] == 0))
        def _():
            tile(kj, masked=True)

        return carry

    lax.fori_loop(0, nk, kv_step, 0)

    for h in range(HEADS):
        cols = slice(h * LANES, (h + 1) * LANES)
        o = (acc_ref[h] * (1.0 / l_ref[h])).T
        o_ref[0, :, cols] = (o * ga_ref[0, :, cols].astype(_f32)).astype(_bf16)


def _out_kernel(x_ref, ya_ref, yb_ref, wo_ref, g_ref, b_ref, o_ref, *, alpha):
    y = jnp.concatenate([ya_ref[0], yb_ref[0]], axis=1)
    z = alpha * x_ref[0] + _dot(y, wo_ref[...])
    mu = jnp.mean(z, axis=-1, keepdims=True)
    zc = z - mu
    var = jnp.mean(zc * zc, axis=-1, keepdims=True)
    o_ref[0] = zc * lax.rsqrt(var + LN_EPS) * g_ref[...] + b_ref[...]


def _rotate_half_cols(w):
    half = w.shape[-1] // 2
    return jnp.concatenate([-w[..., half:], w[..., :half]], axis=-1)


def _prepare_weights(w_in, w_uq, w_ukv, pool_w):
    splits = np.cumsum([Q_RANK, KV_RANK, ROPE, MLA_WIDTH, POOL_WIDTH, POOL_WIDTH])[:-1]
    wq_l, wkv_l, wkr, wga, wu, wgb = jnp.split(w_in, [int(c) for c in splits], axis=1)
    w1 = jnp.concatenate([wq_l, wkv_l, wkr, _rotate_half_cols(wkr), wga, wu, wgb], axis=1)

    wq3 = w_uq.reshape(Q_RANK, HEADS, QK_DIM)
    wq_nope = wq3[:, :, :NOPE].reshape(Q_RANK, HEADS * NOPE)
    wq_rope = wq3[:, :, NOPE:]
    wq_pair = jnp.concatenate([wq_rope, _rotate_half_cols(wq_rope)], axis=-1)
    wqt = jnp.concatenate([wq_nope, wq_pair.reshape(Q_RANK, HEADS * 2 * ROPE)], axis=1).T

    wkv3 = w_ukv.reshape(KV_RANK, HEADS, NOPE + V_DIM)
    wkn = wkv3[:, :, :NOPE].reshape(KV_RANK, HEADS * NOPE)
    wvt = wkv3[:, :, NOPE:].reshape(KV_RANK, HEADS * V_DIM).T

    z = jnp.zeros((POOL_GROUP, POOL_GROUP), pool_w.dtype)
    wp = jnp.stack([jnp.block([[pool_w[0], z], [z, pool_w[1]]]),
                    jnp.block([[pool_w[2], z], [z, pool_w[3]]])])
    return (w1.astype(_bf16), wqt.astype(_bf16), wkn.astype(_bf16), wvt.astype(_bf16),
            wp.astype(_bf16))


def _const_spec(shape):
    return pl.BlockSpec(shape, lambda *_: (0,) * len(shape))


def _layer(x, positions, w1, q_norm_g, wqt, kv_norm_g, wkn, wvt, wp, pool_scale, w_out,
           ln_g, ln_b):
    B, S, D = x.shape
    tm, tq, tk = TOKEN_TILE, Q_TILE, K_TILE
    ns, nq, nk = S // tm, S // tq, S // tk

    half = ROPE // 2
    inv_freq = ROPE_THETA ** (-jnp.arange(half, dtype=_f32) / half)
    freq_row = jnp.tile(inv_freq, LANES // half).reshape(1, LANES)
    pos_col = positions.reshape(B, S, 1)
    pos_row = positions.reshape(B, 1, S)

    def tok_spec(width):
        return pl.BlockSpec((1, tm, width), lambda b, s: (b, s, 0))

    def feat_spec(width):
        return pl.BlockSpec((1, width, tm), lambda b, s: (b, 0, s))

    act = lambda width: jax.ShapeDtypeStruct((B, S, width), _bf16)
    act_t = lambda width: jax.ShapeDtypeStruct((B, width, S), _bf16)
    qnt, qrt, kn, kd, vt, ga, yb = pl.pallas_call(
        _proj_kernel,
        grid=(B, ns),
        in_specs=[tok_spec(D), tok_spec(1), _const_spec((1, LANES)),
                  _const_spec(w1.shape), _const_spec((1, Q_RANK)), _const_spec(wqt.shape),
                  _const_spec((1, KV_RANK)), _const_spec(wkn.shape), _const_spec(wvt.shape),
                  _const_spec(wp.shape), _const_spec((1, POOL_WIDTH))],
        out_specs=[feat_spec(MLA_WIDTH), feat_spec(MLA_WIDTH), tok_spec(MLA_WIDTH),
                   tok_spec(LANES), feat_spec(MLA_WIDTH), tok_spec(MLA_WIDTH),
                   tok_spec(POOL_WIDTH)],
        out_shape=[act_t(MLA_WIDTH), act_t(MLA_WIDTH), act(MLA_WIDTH), act(LANES),
                   act_t(MLA_WIDTH), act(MLA_WIDTH), act(POOL_WIDTH)],
        scratch_shapes=[pltpu.VMEM((MAX_WINDOW, POOL_WIDTH), _f32)],
        compiler_params=pltpu.CompilerParams(
            dimension_semantics=("arbitrary", "arbitrary"),
            vmem_limit_bytes=VMEM_LIMIT_BYTES),
        name="mla_pool_proj",
    )(x, pos_col, freq_row, w1, q_norm_g.reshape(1, -1), wqt, kv_norm_g.reshape(1, -1), wkn,
      wvt, wp, pool_scale.reshape(1, -1))

    pq_blk = positions.reshape(B, nq, tq)
    pk_blk = positions.reshape(B, nk, tk)
    need = pq_blk.max(-1)[:, :, None] >= pk_blk.min(-1)[:, None, :]
    full = pq_blk.min(-1)[:, :, None] >= pk_blk.max(-1)[:, None, :]

    q_spec = pl.BlockSpec((1, tq, MLA_WIDTH), lambda b, i, *_: (b, i, 0))
    qt_spec = pl.BlockSpec((1, MLA_WIDTH, tq), lambda b, i, *_: (b, 0, i))
    k_spec = lambda width: pl.BlockSpec((1, S, width), lambda b, i, *_: (b, 0, 0))
    ya = pl.pallas_call(
        functools.partial(_attn_kernel, nq=nq, nk=nk),
        grid_spec=pltpu.PrefetchScalarGridSpec(
            num_scalar_prefetch=2,
            grid=(B, nq),
            in_specs=[qt_spec, qt_spec, k_spec(MLA_WIDTH), k_spec(LANES),
                      pl.BlockSpec((1, MLA_WIDTH, S), lambda b, i, *_: (b, 0, 0)),
                      q_spec,
                      pl.BlockSpec((1, 1, tq), lambda b, i, *_: (b, 0, i)),
                      k_spec(1)],
            out_specs=q_spec,
            scratch_shapes=[pltpu.VMEM((HEADS, 1, tq), _f32),
                            pltpu.VMEM((HEADS, 1, tq), _f32),
                            pltpu.VMEM((HEADS, V_DIM, tq), _f32)]),
        out_shape=act(MLA_WIDTH),
        compiler_params=pltpu.CompilerParams(
            dimension_semantics=("arbitrary", "arbitrary"),
            vmem_limit_bytes=VMEM_LIMIT_BYTES),
        name="mla_attention",
    )(need.reshape(-1).astype(jnp.int32), full.reshape(-1).astype(jnp.int32),
      qnt, qrt, kn, kd, vt, ga, pos_row, pos_col)

    alpha = (2.0 * DEPTH) ** 0.25
    return pl.pallas_call(
        functools.partial(_out_kernel, alpha=alpha),
        grid=(B, ns),
        in_specs=[tok_spec(D), tok_spec(MLA_WIDTH), tok_spec(POOL_WIDTH),
                  _const_spec(w_out.shape), _const_spec((1, D)), _const_spec((1, D))],
        out_specs=tok_spec(D),
        out_shape=jax.ShapeDtypeStruct((B, S, D), x.dtype),
        compiler_params=pltpu.CompilerParams(
            dimension_semantics=("arbitrary", "arbitrary"),
            vmem_limit_bytes=VMEM_LIMIT_BYTES),
        name="out_proj_layernorm",
    )(x, ya, yb, w_out, ln_g.reshape(1, -1), ln_b.reshape(1, -1))


def kernel(x, positions, w_in, q_norm_g, w_uq, kv_norm_g, w_ukv, pool_w, pool_scale, w_out,
           ln_g, ln_b):
    w1, wqt, wkn, wvt, wp = _prepare_weights(w_in, w_uq, w_ukv, pool_w)
    w_out_b = w_out.astype(_bf16)
    for layer in range(DEPTH):
        x = _layer(x, positions, w1, q_norm_g, wqt, kv_norm_g, wkn, wvt, wp, pool_scale,
                   w_out_b, ln_g[layer], ln_b[layer])
    return x
```

```python
import functools
import math

import jax
import jax.numpy as jnp
import numpy as np
from jax import lax
from jax.experimental import pallas as pl
from jax.experimental.pallas import tpu as pltpu

D_MODEL = 1024
DEPTH = 1
HEADS = 4
NOPE = 128
ROPE = 64
QK_DIM = NOPE + ROPE
V_DIM = 128
Q_RANK = 512
KV_RANK = 256
MLA_WIDTH = HEADS * V_DIM
POOL_WINDOWS = (2, 4, 8, 16)
POOL_GROUP = 128
POOL_WIDTH = len(POOL_WINDOWS) * POOL_GROUP
ROPE_THETA = 10000.0
RMS_EPS = 1e-6
LN_EPS = 1e-5

LANES = 128
MAX_WINDOW = max(POOL_WINDOWS)
VMEM_LIMIT_BYTES = 56 * 1024 * 1024

_C_Q = 0
_C_KV = _C_Q + Q_RANK
_C_KR = _C_KV + KV_RANK
_C_GA = _C_KR + 2 * ROPE
_C_U = _C_GA + MLA_WIDTH
_C_GB = _C_U + POOL_WIDTH
_C_END = _C_GB + POOL_WIDTH

_SCORE_SCALE = (QK_DIM ** -0.5) * math.log2(math.e)
_MASKED = -1e30

TOKEN_TILE = 1024
Q_TILE = 512
K_TILE = 512
S_CHUNK = 32

_bf16 = jnp.bfloat16
_f32 = jnp.float32


def _dot(a, b):
    return jnp.dot(a, b, preferred_element_type=_f32)


def _rms_norm(h, g):
    ms = jnp.mean(h * h, axis=-1, keepdims=True)
    return (h * lax.rsqrt(ms + RMS_EPS)) * g


def _silu(z):
    return z * (1.0 / (1.0 + jnp.exp(-z)))


def _dot_nt(a, b):
    return lax.dot_general(a, b, (((1,), (1,)), ((), ())), preferred_element_type=_f32)


def _proj_kernel(x_ref, pos_ref, freq_ref, w1_ref, qg_ref, wqt_ref, kvg_ref, wkn_ref, wvt_ref,
                 wp_ref, ps_ref,
                 qnt_ref, qrt_ref, kn_ref, kd_ref, vt_ref, ga_ref, yb_ref,
                 carry_ref):
    tm = x_ref.shape[1]
    si = pl.program_id(1)
    xb = x_ref[0].astype(_bf16)

    ang_t = freq_ref[...] * pos_ref[0].astype(_f32)
    cos_t, sin_t = jnp.cos(ang_t), jnp.sin(ang_t)
    tab_t = jnp.concatenate([cos_t, cos_t, sin_t, sin_t], axis=0)
    tab = tab_t.T
    qtab_t = tab_t * _SCORE_SCALE

    xqn = _rms_norm(_dot(xb, w1_ref[:, _C_Q:_C_KV]), qg_ref[...])
    qt = _dot_nt(wqt_ref[...], xqn.astype(_bf16))
    qnt_ref[0] = (qt[:HEADS * NOPE] * _SCORE_SCALE).astype(_bf16)
    for h in range(HEADS):
        lo = HEADS * NOPE + h * LANES
        qrt_ref[0, h * LANES:(h + 1) * LANES, :] = (qt[lo:lo + LANES] * qtab_t).astype(_bf16)

    xkvn = _rms_norm(_dot(xb, w1_ref[:, _C_KV:_C_KR]), kvg_ref[...]).astype(_bf16)
    kn_ref[0] = _dot(xkvn, wkn_ref[...]).astype(_bf16)
    vt_ref[0] = _dot_nt(wvt_ref[...], xkvn).astype(_bf16)
    t = _dot(xb, w1_ref[:, _C_KR:_C_GA]) * tab
    kd_ref[0] = (t + pltpu.roll(t, ROPE, axis=1)).astype(_bf16)

    ga_ref[0] = _silu(_dot(xb, w1_ref[:, _C_GA:_C_U])).astype(_bf16)

    u = _dot(xb, w1_ref[:, _C_U:_C_GB])

    @pl.when(si == 0)
    def _():
        carry_ref[...] = jnp.zeros_like(carry_ref)

    ext = jnp.concatenate([carry_ref[...], u], axis=0)
    carry_ref[...] = u[tm - MAX_WINDOW:, :]
    sums = []
    acc = ext
    width = 1
    for g, w in enumerate(POOL_WINDOWS):
        while width < w:
            acc = acc + pltpu.roll(acc, width, axis=0)
            width *= 2
        sums.append(acc[MAX_WINDOW:, :POOL_GROUP])
        acc = acc[:, POOL_GROUP:]
    tok = si * tm + lax.broadcasted_iota(jnp.int32, (tm, 1), 0)
    pooled = []
    for g, w in enumerate(POOL_WINDOWS):
        inv_cnt = 1.0 / jnp.minimum(tok + 1, w).astype(_f32)
        pooled.append(sums[g] * inv_cnt - u[:, g * POOL_GROUP:(g + 1) * POOL_GROUP])
    mixed = jnp.concatenate(
        [_dot(jnp.concatenate(pooled[0:2], axis=1).astype(_bf16), wp_ref[0]),
         _dot(jnp.concatenate(pooled[2:4], axis=1).astype(_bf16), wp_ref[1])], axis=1)
    gb = _silu(_dot(xb, w1_ref[:, _C_GB:_C_END]))
    yb_ref[0] = (mixed * ps_ref[...] * gb).astype(_bf16)


def _attn_kernel(need_ref, full_ref,
                 qnt_ref, qrt_ref, kn_ref, kd_ref, vt_ref, ga_ref, pq_ref, pk_ref,
                 o_ref,
                 m_ref, l_ref, acc_ref, s_ref, p_ref, *, nq, nk):
    tk = K_TILE
    b = pl.program_id(0)
    qi = pl.program_id(1)

    m_ref[...] = jnp.full(m_ref.shape, _MASKED, _f32)
    l_ref[...] = jnp.zeros_like(l_ref)
    acc_ref[...] = jnp.zeros_like(acc_ref)

    def tile(kj, masked):
        ks = pl.multiple_of(kj * tk, tk)
        kd = kd_ref[0, pl.ds(ks, tk), :]
        if masked:
            keep = pk_ref[0, pl.ds(ks, tk), :] <= pq_ref[0]


        def scores(h):
            rows = slice(h * LANES, (h + 1) * LANES)
            qc = jnp.concatenate([qnt_ref[0, rows, :], qrt_ref[0, rows, :]], axis=0)
            kc = jnp.concatenate([kn_ref[0, pl.ds(ks, tk), rows], kd], axis=1)
            s = _dot(kc, qc)
            s_ref[h % 2] = jnp.where(keep, s, _MASKED) if masked else s

        def accumulate(h):
            rows = slice(h * LANES, (h + 1) * LANES)
            sb = s_ref.at[h % 2]
            chunks = [slice(c, c + S_CHUNK) for c in range(0, tk, S_CHUNK)]
            mx = sb[chunks[0], :]
            for c in chunks[1:]:
                mx = jnp.maximum(mx, sb[c, :])
            m_old = m_ref[h]
            m_new = jnp.maximum(m_old, jnp.max(mx, axis=0, keepdims=True))
            alpha = jnp.exp2(m_old - m_new)
            psum = None
            for c in chunks:
                p = jnp.exp2(sb[c, :] - m_new)
                psum = p if psum is None else psum + p
                p_ref[c, :] = p.astype(_bf16)
            l_ref[h] = alpha * l_ref[h] + jnp.sum(psum, axis=0, keepdims=True)
            acc_ref[h] = alpha * acc_ref[h] + _dot(vt_ref[0, rows, pl.ds(ks, tk)],
                                                   p_ref[...])
            m_ref[h] = m_new

        scores(0)
        for h in range(HEADS):
            if h + 1 < HEADS:
                scores(h + 1)
            accumulate(h)

    def kv_step(kj, carry):
        idx = (b * nq + qi) * nk + kj

        @pl.when(jnp.logical_and(need_ref[idx] != 0, full_ref[idx] != 0))
        def _():
            tile(kj, masked=False)

        @pl.when(jnp.logical_and(need_ref[idx] != 0, full_ref[idx] == 0))
        def _():
            tile(kj, masked=True)

        return carry

    lax.fori_loop(0, nk, kv_step, 0)

    for h in range(HEADS):
        cols = slice(h * LANES, (h + 1) * LANES)
        o = (acc_ref[h] * (1.0 / l_ref[h])).T
        o_ref[0, :, cols] = (o * ga_ref[0, :, cols].astype(_f32)).astype(_bf16)


def _out_kernel(x_ref, ya_ref, yb_ref, wo_ref, g_ref, b_ref, o_ref, *, alpha):
    y = jnp.concatenate([ya_ref[0], yb_ref[0]], axis=1)
    z = alpha * x_ref[0] + _dot(y, wo_ref[...])
    mu = jnp.mean(z, axis=-1, keepdims=True)
    zc = z - mu
    var = jnp.mean(zc * zc, axis=-1, keepdims=True)
    o_ref[0] = zc * lax.rsqrt(var + LN_EPS) * g_ref[...] + b_ref[...]


def _rotate_half_cols(w):
    half = w.shape[-1] // 2
    return jnp.concatenate([-w[..., half:], w[..., :half]], axis=-1)


def _prepare_weights(w_in, w_uq, w_ukv, pool_w):
    splits = np.cumsum([Q_RANK, KV_RANK, ROPE, MLA_WIDTH, POOL_WIDTH, POOL_WIDTH])[:-1]
    wq_l, wkv_l, wkr, wga, wu, wgb = jnp.split(w_in, [int(c) for c in splits], axis=1)
    w1 = jnp.concatenate([wq_l, wkv_l, wkr, _rotate_half_cols(wkr), wga, wu, wgb], axis=1)

    wq3 = w_uq.reshape(Q_RANK, HEADS, QK_DIM)
    wq_nope = wq3[:, :, :NOPE].reshape(Q_RANK, HEADS * NOPE)
    wq_rope = wq3[:, :, NOPE:]
    wq_pair = jnp.concatenate([wq_rope, _rotate_half_cols(wq_rope)], axis=-1)
    wqt = jnp.concatenate([wq_nope, wq_pair.reshape(Q_RANK, HEADS * 2 * ROPE)], axis=1).T

    wkv3 = w_ukv.reshape(KV_RANK, HEADS, NOPE + V_DIM)
    wkn = wkv3[:, :, :NOPE].reshape(KV_RANK, HEADS * NOPE)
    wvt = wkv3[:, :, NOPE:].reshape(KV_RANK, HEADS * V_DIM).T

    z = jnp.zeros((POOL_GROUP, POOL_GROUP), pool_w.dtype)
    wp = jnp.stack([jnp.block([[pool_w[0], z], [z, pool_w[1]]]),
                    jnp.block([[pool_w[2], z], [z, pool_w[3]]])])
    return (w1.astype(_bf16), wqt.astype(_bf16), wkn.astype(_bf16), wvt.astype(_bf16),
            wp.astype(_bf16))


def _const_spec(shape):
    return pl.BlockSpec(shape, lambda *_: (0,) * len(shape))


def _layer(x, positions, w1, q_norm_g, wqt, kv_norm_g, wkn, wvt, wp, pool_scale, w_out,
           ln_g, ln_b):
    B, S, D = x.shape
    tm, tq, tk = TOKEN_TILE, Q_TILE, K_TILE
    ns, nq, nk = S // tm, S // tq, S // tk

    half = ROPE // 2
    inv_freq = ROPE_THETA ** (-jnp.arange(half, dtype=_f32) / half)
    freq_col = inv_freq.reshape(half, 1)
    pos_col = positions.reshape(B, S, 1)
    pos_row = positions.reshape(B, 1, S)

    def tok_spec(width):
        return pl.BlockSpec((1, tm, width), lambda b, s: (b, s, 0))

    def feat_spec(width):
        return pl.BlockSpec((1, width, tm), lambda b, s: (b, 0, s))

    act = lambda width: jax.ShapeDtypeStruct((B, S, width), _bf16)
    act_t = lambda width: jax.ShapeDtypeStruct((B, width, S), _bf16)
    qnt, qrt, kn, kd, vt, ga, yb = pl.pallas_call(
        _proj_kernel,
        grid=(B, ns),
        in_specs=[tok_spec(D), feat_spec(1), _const_spec((half, 1)),
                  _const_spec(w1.shape), _const_spec((1, Q_RANK)), _const_spec(wqt.shape),
                  _const_spec((1, KV_RANK)), _const_spec(wkn.shape), _const_spec(wvt.shape),
                  _const_spec(wp.shape), _const_spec((1, POOL_WIDTH))],
        out_specs=[feat_spec(MLA_WIDTH), feat_spec(MLA_WIDTH), tok_spec(MLA_WIDTH),
                   tok_spec(LANES), feat_spec(MLA_WIDTH), tok_spec(MLA_WIDTH),
                   tok_spec(POOL_WIDTH)],
        out_shape=[act_t(MLA_WIDTH), act_t(MLA_WIDTH), act(MLA_WIDTH), act(LANES),
                   act_t(MLA_WIDTH), act(MLA_WIDTH), act(POOL_WIDTH)],
        scratch_shapes=[pltpu.VMEM((MAX_WINDOW, POOL_WIDTH), _f32)],
        compiler_params=pltpu.CompilerParams(
            dimension_semantics=("arbitrary", "arbitrary"),
            vmem_limit_bytes=VMEM_LIMIT_BYTES),
        name="mla_pool_proj",
    )(x, pos_row, freq_col, w1, q_norm_g.reshape(1, -1), wqt, kv_norm_g.reshape(1, -1), wkn,
      wvt, wp, pool_scale.reshape(1, -1))

    pq_blk = positions.reshape(B, nq, tq)
    pk_blk = positions.reshape(B, nk, tk)
    need = pq_blk.max(-1)[:, :, None] >= pk_blk.min(-1)[:, None, :]
    full = pq_blk.min(-1)[:, :, None] >= pk_blk.max(-1)[:, None, :]

    q_spec = pl.BlockSpec((1, tq, MLA_WIDTH), lambda b, i, *_: (b, i, 0))
    qt_spec = pl.BlockSpec((1, MLA_WIDTH, tq), lambda b, i, *_: (b, 0, i))
    k_spec = lambda width: pl.BlockSpec((1, S, width), lambda b, i, *_: (b, 0, 0))
    ya = pl.pallas_call(
        functools.partial(_attn_kernel, nq=nq, nk=nk),
        grid_spec=pltpu.PrefetchScalarGridSpec(
            num_scalar_prefetch=2,
            grid=(B, nq),
            in_specs=[qt_spec, qt_spec, k_spec(MLA_WIDTH), k_spec(LANES),
                      pl.BlockSpec((1, MLA_WIDTH, S), lambda b, i, *_: (b, 0, 0)),
                      q_spec,
                      pl.BlockSpec((1, 1, tq), lambda b, i, *_: (b, 0, i)),
                      k_spec(1)],
            out_specs=q_spec,
            scratch_shapes=[pltpu.VMEM((HEADS, 1, tq), _f32),
                            pltpu.VMEM((HEADS, 1, tq), _f32),
                            pltpu.VMEM((HEADS, V_DIM, tq), _f32),
                            pltpu.VMEM((2, tk, tq), _f32),
                            pltpu.VMEM((tk, tq), _bf16)]),
        out_shape=act(MLA_WIDTH),
        compiler_params=pltpu.CompilerParams(
            dimension_semantics=("arbitrary", "arbitrary"),
            vmem_limit_bytes=VMEM_LIMIT_BYTES),
        name="mla_attention",
    )(need.reshape(-1).astype(jnp.int32), full.reshape(-1).astype(jnp.int32),
      qnt, qrt, kn, kd, vt, ga, pos_row, pos_col)

    alpha = (2.0 * DEPTH) ** 0.25
    return pl.pallas_call(
        functools.partial(_out_kernel, alpha=alpha),
        grid=(B, ns),
        in_specs=[tok_spec(D), tok_spec(MLA_WIDTH), tok_spec(POOL_WIDTH),
                  _const_spec(w_out.shape), _const_spec((1, D)), _const_spec((1, D))],
        out_specs=tok_spec(D),
        out_shape=jax.ShapeDtypeStruct((B, S, D), x.dtype),
        compiler_params=pltpu.CompilerParams(
            dimension_semantics=("arbitrary", "arbitrary"),
            vmem_limit_bytes=VMEM_LIMIT_BYTES),
        name="out_proj_layernorm",
    )(x, ya, yb, w_out, ln_g.reshape(1, -1), ln_b.reshape(1, -1))


def kernel(x, positions, w_in, q_norm_g, w_uq, kv_norm_g, w_ukv, pool_w, pool_scale, w_out,
           ln_g, ln_b):
    w1, wqt, wkn, wvt, wp = _prepare_weights(w_in, w_uq, w_ukv, pool_w)
    w_out_b = w_out.astype(_bf16)
    for layer in range(DEPTH):
        x = _layer(x, positions, w1, q_norm_g, wqt, kv_norm_g, wkn, wvt, wp, pool_scale,
                   w_out_b, ln_g[layer], ln_b[layer])
    return x
```

```python
import functools
import math

import jax
import jax.numpy as jnp
import numpy as np
from jax import lax
from jax.experimental import pallas as pl
from jax.experimental.pallas import tpu as pltpu

D_MODEL = 1024
DEPTH = 1
HEADS = 4
NOPE = 128
ROPE = 64
QK_DIM = NOPE + ROPE
V_DIM = 128
Q_RANK = 512
KV_RANK = 256
MLA_WIDTH = HEADS * V_DIM
POOL_WINDOWS = (2, 4, 8, 16)
POOL_GROUP = 128
POOL_WIDTH = len(POOL_WINDOWS) * POOL_GROUP
ROPE_THETA = 10000.0
RMS_EPS = 1e-6
LN_EPS = 1e-5

LANES = 128
MAX_WINDOW = max(POOL_WINDOWS)
VMEM_LIMIT_BYTES = 56 * 1024 * 1024

_C_Q = 0
_C_KV = _C_Q + Q_RANK
_C_KR = _C_KV + KV_RANK
_C_GA = _C_KR + 2 * ROPE
_C_U = _C_GA + MLA_WIDTH
_C_GB = _C_U + POOL_WIDTH
_C_END = _C_GB + POOL_WIDTH

_SCORE_SCALE = (QK_DIM ** -0.5) * math.log2(math.e)
_MASKED = -1e30

TOKEN_TILE = 1024
Q_TILE = 512
K_TILE = 512
S_CHUNK = 32
L_ROWS = 16

_bf16 = jnp.bfloat16
_f32 = jnp.float32


def _dot(a, b):
    return jnp.dot(a, b, preferred_element_type=_f32)


def _rms_norm(h, g):
    ms = jnp.mean(h * h, axis=-1, keepdims=True)
    return (h * lax.rsqrt(ms + RMS_EPS)) * g


def _silu(z):
    return z * (1.0 / (1.0 + jnp.exp(-z)))


def _dot_nt(a, b):
    return lax.dot_general(a, b, (((1,), (1,)), ((), ())), preferred_element_type=_f32)


def _proj_kernel(x_ref, pos_ref, freq_ref, w1_ref, qg_ref, wqt_ref, kvg_ref, wkn_ref, wvt_ref,
                 wp_ref, ps_ref,
                 qnt_ref, qrt_ref, kn_ref, kd_ref, vt_ref, ga_ref, yb_ref,
                 carry_ref):
    tm = x_ref.shape[1]
    si = pl.program_id(1)
    xb = x_ref[0].astype(_bf16)

    ang_t = freq_ref[...] * pos_ref[0].astype(_f32)
    cos_t, sin_t = jnp.cos(ang_t), jnp.sin(ang_t)
    tab_t = jnp.concatenate([cos_t, cos_t, sin_t, sin_t], axis=0)
    tab = tab_t.T
    qtab_t = tab_t * _SCORE_SCALE

    xqn = _rms_norm(_dot(xb, w1_ref[:, _C_Q:_C_KV]), qg_ref[...])
    qt = _dot_nt(wqt_ref[...], xqn.astype(_bf16))
    qnt_ref[0] = (qt[:HEADS * NOPE] * _SCORE_SCALE).astype(_bf16)
    for h in range(HEADS):
        lo = HEADS * NOPE + h * LANES
        qrt_ref[0, h * LANES:(h + 1) * LANES, :] = (qt[lo:lo + LANES] * qtab_t).astype(_bf16)

    xkvn = _rms_norm(_dot(xb, w1_ref[:, _C_KV:_C_KR]), kvg_ref[...]).astype(_bf16)
    kn_ref[0] = _dot(xkvn, wkn_ref[...]).astype(_bf16)
    vt_ref[0] = _dot_nt(wvt_ref[...], xkvn).astype(_bf16)
    t = _dot(xb, w1_ref[:, _C_KR:_C_GA]) * tab
    kd_ref[0] = (t + pltpu.roll(t, ROPE, axis=1)).astype(_bf16)

    ga_ref[0] = _silu(_dot(xb, w1_ref[:, _C_GA:_C_U])).astype(_bf16)

    u = _dot(xb, w1_ref[:, _C_U:_C_GB])

    @pl.when(si == 0)
    def _():
        carry_ref[...] = jnp.zeros_like(carry_ref)

    ext = jnp.concatenate([carry_ref[...], u], axis=0)
    carry_ref[...] = u[tm - MAX_WINDOW:, :]
    sums = []
    acc = ext
    width = 1
    for g, w in enumerate(POOL_WINDOWS):
        while width < w:
            acc = acc + pltpu.roll(acc, width, axis=0)
            width *= 2
        sums.append(acc[MAX_WINDOW:, :POOL_GROUP])
        acc = acc[:, POOL_GROUP:]
    tok = si * tm + lax.broadcasted_iota(jnp.int32, (tm, 1), 0)
    pooled = []
    for g, w in enumerate(POOL_WINDOWS):
        inv_cnt = 1.0 / jnp.minimum(tok + 1, w).astype(_f32)
        pooled.append(sums[g] * inv_cnt - u[:, g * POOL_GROUP:(g + 1) * POOL_GROUP])
    mixed = jnp.concatenate(
        [_dot(jnp.concatenate(pooled[0:2], axis=1).astype(_bf16), wp_ref[0]),
         _dot(jnp.concatenate(pooled[2:4], axis=1).astype(_bf16), wp_ref[1])], axis=1)
    gb = _silu(_dot(xb, w1_ref[:, _C_GB:_C_END]))
    yb_ref[0] = (mixed * ps_ref[...] * gb).astype(_bf16)


def _attn_kernel(qstart_ref, kstart_ref, nfull_ref, nmask_ref,
                 qnt_ref, qrt_ref, kn_ref, kd_ref, vt_ref, ga_ref, pq_ref, pk_ref,
                 o_ref,
                 m_ref, alpha_ref, acc_ref, s_ref, p_ref, bias_ref, *, npairs):
    tq, tk = Q_TILE, K_TILE
    nq = o_ref.shape[1] // tq
    b = pl.program_id(0)
    n_full = nfull_ref[b]
    n_all = n_full + nmask_ref[b]

    m_ref[...] = jnp.full(m_ref.shape, _MASKED, _f32)
    acc_ref[...] = jnp.zeros_like(acc_ref)

    def pair(i):
        qs = pl.multiple_of(qstart_ref[b * npairs + i], tq)
        ks = pl.multiple_of(kstart_ref[b * npairs + i], tk)
        return qs, ks

    chunks = [slice(c, c + S_CHUNK) for c in range(0, tk, S_CHUNK)]
    ones_rows = jnp.ones((L_ROWS, tk), _bf16)

    def make_bias(qs, ks):
        pq = pq_ref[0, :, pl.ds(qs, tq)]
        for c in chunks:
            keep = pk_ref[0, pl.ds(ks + c.start, S_CHUNK), :] <= pq
            bias_ref[c, :] = jnp.where(keep, 0.0, _MASKED)

    def scores(h, qs, ks, masked):
        rows = slice(h * LANES, (h + 1) * LANES)
        qc = jnp.concatenate([qnt_ref[0, rows, pl.ds(qs, tq)],
                              qrt_ref[0, rows, pl.ds(qs, tq)]], axis=0)
        kc = jnp.concatenate([kn_ref[0, pl.ds(ks, tk), rows],
                              kd_ref[0, pl.ds(ks, tk), :]], axis=1)
        s = _dot(kc, qc)
        s_ref[h] = s + bias_ref[...] if masked else s

    def softmax(h, qs):
        cols = pl.ds(qs, tq)
        sb = s_ref.at[h]
        mx = sb[chunks[0], :]
        for c in chunks[1:]:
            mx = jnp.maximum(mx, sb[c, :])
        m_old = m_ref[h, :, cols]
        m_new = jnp.maximum(m_old, jnp.max(mx, axis=0, keepdims=True))
        alpha_ref[h] = jnp.exp2(m_old - m_new)
        m_ref[h, :, cols] = m_new
        for c in chunks:
            p_ref[h, c, :] = jnp.exp2(sb[c, :] - m_new).astype(_bf16)

    def pv(h, qs, ks):
        rows = slice(h * LANES, (h + 1) * LANES)
        cols = pl.ds(qs, tq)
        v_ext = jnp.concatenate([vt_ref[0, rows, pl.ds(ks, tk)], ones_rows], axis=0)
        acc_ref[h, :, cols] = alpha_ref[h] * acc_ref[h, :, cols] + _dot(v_ext, p_ref[h])

    def step(i, next_masked):
        qs, ks = pair(i)
        if next_masked is not None:
            qs_n, ks_n = pair(i + 1)
            if next_masked:
                make_bias(qs_n, ks_n)
        for h in range(HEADS):
            pv(h, qs, ks)
            if next_masked is not None:
                scores(h, qs_n, ks_n, next_masked)
            if h + 1 < HEADS:
                softmax(h + 1, qs)
        if next_masked is not None:
            softmax(0, qs_n)

    qs0, ks0 = pair(0)

    @pl.when(n_full > 0)
    def _():
        for h in range(HEADS):
            scores(h, qs0, ks0, False)

    @pl.when(n_full == 0)
    def _():
        make_bias(qs0, ks0)
        for h in range(HEADS):
            scores(h, qs0, ks0, True)

    softmax(0, qs0)

    def step_next_full(i, carry):
        step(i, False)
        return carry

    def step_next_masked(i, carry):
        step(i, True)
        return carry

    first_masked_next = jnp.maximum(n_full - 1, 0)
    lax.fori_loop(0, first_masked_next, step_next_full, 0)
    lax.fori_loop(first_masked_next, n_all - 1, step_next_masked, 0)
    step(n_all - 1, None)

    for qt in range(nq):
        rows = slice(qt * tq, (qt + 1) * tq)
        for h in range(HEADS):
            cols = slice(h * LANES, (h + 1) * LANES)
            inv_l = 1.0 / acc_ref[h, V_DIM:V_DIM + 1, rows]
            o = (acc_ref[h, :V_DIM, rows] * inv_l).T
            o_ref[0, rows, cols] = (o * ga_ref[0, rows, cols].astype(_f32)).astype(_bf16)


def _out_kernel(x_ref, ya_ref, yb_ref, wo_ref, g_ref, b_ref, o_ref, *, alpha):
    y = jnp.concatenate([ya_ref[0], yb_ref[0]], axis=1)
    z = alpha * x_ref[0] + _dot(y, wo_ref[...])
    mu = jnp.mean(z, axis=-1, keepdims=True)
    zc = z - mu
    var = jnp.mean(zc * zc, axis=-1, keepdims=True)
    o_ref[0] = zc * lax.rsqrt(var + LN_EPS) * g_ref[...] + b_ref[...]


def _rotate_half_cols(w):
    half = w.shape[-1] // 2
    return jnp.concatenate([-w[..., half:], w[..., :half]], axis=-1)


def _prepare_weights(w_in, w_uq, w_ukv, pool_w):
    splits = np.cumsum([Q_RANK, KV_RANK, ROPE, MLA_WIDTH, POOL_WIDTH, POOL_WIDTH])[:-1]
    wq_l, wkv_l, wkr, wga, wu, wgb = jnp.split(w_in, [int(c) for c in splits], axis=1)
    w1 = jnp.concatenate([wq_l, wkv_l, wkr, _rotate_half_cols(wkr), wga, wu, wgb], axis=1)

    wq3 = w_uq.reshape(Q_RANK, HEADS, QK_DIM)
    wq_nope = wq3[:, :, :NOPE].reshape(Q_RANK, HEADS * NOPE)
    wq_rope = wq3[:, :, NOPE:]
    wq_pair = jnp.concatenate([wq_rope, _rotate_half_cols(wq_rope)], axis=-1)
    wqt = jnp.concatenate([wq_nope, wq_pair.reshape(Q_RANK, HEADS * 2 * ROPE)], axis=1).T

    wkv3 = w_ukv.reshape(KV_RANK, HEADS, NOPE + V_DIM)
    wkn = wkv3[:, :, :NOPE].reshape(KV_RANK, HEADS * NOPE)
    wvt = wkv3[:, :, NOPE:].reshape(KV_RANK, HEADS * V_DIM).T

    z = jnp.zeros((POOL_GROUP, POOL_GROUP), pool_w.dtype)
    wp = jnp.stack([jnp.block([[pool_w[0], z], [z, pool_w[1]]]),
                    jnp.block([[pool_w[2], z], [z, pool_w[3]]])])
    return (w1.astype(_bf16), wqt.astype(_bf16), wkn.astype(_bf16), wvt.astype(_bf16),
            wp.astype(_bf16))


def _const_spec(shape):
    return pl.BlockSpec(shape, lambda *_: (0,) * len(shape))


def _layer(x, positions, w1, q_norm_g, wqt, kv_norm_g, wkn, wvt, wp, pool_scale, w_out,
           ln_g, ln_b):
    B, S, D = x.shape
    tm, tq, tk = TOKEN_TILE, Q_TILE, K_TILE
    ns, nq, nk = S // tm, S // tq, S // tk

    half = ROPE // 2
    inv_freq = ROPE_THETA ** (-jnp.arange(half, dtype=_f32) / half)
    freq_col = inv_freq.reshape(half, 1)
    pos_col = positions.reshape(B, S, 1)
    pos_row = positions.reshape(B, 1, S)

    def tok_spec(width):
        return pl.BlockSpec((1, tm, width), lambda b, s: (b, s, 0))

    def feat_spec(width):
        return pl.BlockSpec((1, width, tm), lambda b, s: (b, 0, s))

    act = lambda width: jax.ShapeDtypeStruct((B, S, width), _bf16)
    act_t = lambda width: jax.ShapeDtypeStruct((B, width, S), _bf16)
    qnt, qrt, kn, kd, vt, ga, yb = pl.pallas_call(
        _proj_kernel,
        grid=(B, ns),
        in_specs=[tok_spec(D), feat_spec(1), _const_spec((half, 1)),
                  _const_spec(w1.shape), _const_spec((1, Q_RANK)), _const_spec(wqt.shape),
                  _const_spec((1, KV_RANK)), _const_spec(wkn.shape), _const_spec(wvt.shape),
                  _const_spec(wp.shape), _const_spec((1, POOL_WIDTH))],
        out_specs=[feat_spec(MLA_WIDTH), feat_spec(MLA_WIDTH), tok_spec(MLA_WIDTH),
                   tok_spec(LANES), feat_spec(MLA_WIDTH), tok_spec(MLA_WIDTH),
                   tok_spec(POOL_WIDTH)],
        out_shape=[act_t(MLA_WIDTH), act_t(MLA_WIDTH), act(MLA_WIDTH), act(LANES),
                   act_t(MLA_WIDTH), act(MLA_WIDTH), act(POOL_WIDTH)],
        scratch_shapes=[pltpu.VMEM((MAX_WINDOW, POOL_WIDTH), _f32)],
        compiler_params=pltpu.CompilerParams(
            dimension_semantics=("arbitrary", "arbitrary"),
            vmem_limit_bytes=VMEM_LIMIT_BYTES),
        name="mla_pool_proj",
    )(x, pos_row, freq_col, w1, q_norm_g.reshape(1, -1), wqt, kv_norm_g.reshape(1, -1), wkn,
      wvt, wp, pool_scale.reshape(1, -1))

    pq_blk = positions.reshape(B, nq, tq)
    pk_blk = positions.reshape(B, nk, tk)
    need = pq_blk.max(-1)[:, :, None] >= pk_blk.min(-1)[:, None, :]
    full = pq_blk.min(-1)[:, :, None] >= pk_blk.max(-1)[:, None, :]
    npairs = nq * nk
    rank = jnp.where(need & full, 0, jnp.where(need, 1, 2)).reshape(B, npairs)
    order = jnp.argsort(rank, axis=1, stable=True).astype(jnp.int32)
    q_start = (order // nk) * tq
    k_start = (order % nk) * tk
    n_full = jnp.sum(rank == 0, axis=1, dtype=jnp.int32)
    n_mask = jnp.sum(rank == 1, axis=1, dtype=jnp.int32)

    row_spec = lambda d1, d2: pl.BlockSpec((1, d1, d2), lambda b, *_: (b, 0, 0))
    ya = pl.pallas_call(
        functools.partial(_attn_kernel, npairs=npairs),
        grid_spec=pltpu.PrefetchScalarGridSpec(
            num_scalar_prefetch=4,
            grid=(B,),
            in_specs=[row_spec(MLA_WIDTH, S), row_spec(MLA_WIDTH, S), row_spec(S, MLA_WIDTH),
                      row_spec(S, LANES), row_spec(MLA_WIDTH, S), row_spec(S, MLA_WIDTH),
                      row_spec(1, S), row_spec(S, 1)],
            out_specs=row_spec(S, MLA_WIDTH),
            scratch_shapes=[pltpu.VMEM((HEADS, 1, S), _f32),
                            pltpu.VMEM((HEADS, 1, tq), _f32),
                            pltpu.VMEM((HEADS, V_DIM + L_ROWS, S), _f32),
                            pltpu.VMEM((HEADS, tk, tq), _f32),
                            pltpu.VMEM((HEADS, tk, tq), _bf16),
                            pltpu.VMEM((tk, tq), _f32)]),
        out_shape=act(MLA_WIDTH),
        compiler_params=pltpu.CompilerParams(
            dimension_semantics=("arbitrary",),
            vmem_limit_bytes=VMEM_LIMIT_BYTES),
        name="mla_attention",
    )(q_start.reshape(-1), k_start.reshape(-1), n_full, n_mask,
      qnt, qrt, kn, kd, vt, ga, pos_row, pos_col)

    alpha = (2.0 * DEPTH) ** 0.25
    return pl.pallas_call(
        functools.partial(_out_kernel, alpha=alpha),
        grid=(B, ns),
        in_specs=[tok_spec(D), tok_spec(MLA_WIDTH), tok_spec(POOL_WIDTH),
                  _const_spec(w_out.shape), _const_spec((1, D)), _const_spec((1, D))],
        out_specs=tok_spec(D),
        out_shape=jax.ShapeDtypeStruct((B, S, D), x.dtype),
        compiler_params=pltpu.CompilerParams(
            dimension_semantics=("arbitrary", "arbitrary"),
            vmem_limit_bytes=VMEM_LIMIT_BYTES),
        name="out_proj_layernorm",
    )(x, ya, yb, w_out, ln_g.reshape(1, -1), ln_b.reshape(1, -1))


def kernel(x, positions, w_in, q_norm_g, w_uq, kv_norm_g, w_ukv, pool_w, pool_scale, w_out,
           ln_g, ln_b):
    w1, wqt, wkn, wvt, wp = _prepare_weights(w_in, w_uq, w_ukv, pool_w)
    w_out_b = w_out.astype(_bf16)
    for layer in range(DEPTH):
        x = _layer(x, positions, w1, q_norm_g, wqt, kv_norm_g, wkn, wvt, wp, pool_scale,
                   w_out_b, ln_g[layer], ln_b[layer])
    return x
```

```python
import functools
import math

import jax
import jax.numpy as jnp
import numpy as np
from jax import lax
from jax.experimental import pallas as pl
from jax.experimental.pallas import tpu as pltpu

D_MODEL = 1024
DEPTH = 1
HEADS = 4
NOPE = 128
ROPE = 64
QK_DIM = NOPE + ROPE
V_DIM = 128
Q_RANK = 512
KV_RANK = 256
MLA_WIDTH = HEADS * V_DIM
POOL_WINDOWS = (2, 4, 8, 16)
POOL_GROUP = 128
POOL_WIDTH = len(POOL_WINDOWS) * POOL_GROUP
ROPE_THETA = 10000.0
RMS_EPS = 1e-6
LN_EPS = 1e-5

LANES = 128
MAX_WINDOW = max(POOL_WINDOWS)
VMEM_LIMIT_BYTES = 56 * 1024 * 1024

_C_Q = 0
_C_KV = _C_Q + Q_RANK
_C_KR = _C_KV + KV_RANK
_C_GA = _C_KR + 2 * ROPE
_C_U = _C_GA + MLA_WIDTH
_C_GB = _C_U + POOL_WIDTH
_C_END = _C_GB + POOL_WIDTH

_SCORE_SCALE = (QK_DIM ** -0.5) * math.log2(math.e)
_MASKED = -1e30

TOKEN_TILE = 1024
Q_TILE = 512
K_TILE = 512
S_CHUNK = 32
PROJ_SUBTILE = 256
OUT_SUBTILE = 256
L_ROWS = 16

_bf16 = jnp.bfloat16
_f32 = jnp.float32


def _dot(a, b):
    return jnp.dot(a, b, preferred_element_type=_f32)


def _rms_norm(h, g):
    ms = jnp.mean(h * h, axis=-1, keepdims=True)
    return (h * lax.rsqrt(ms + RMS_EPS)) * g


def _silu(z):
    return z * (1.0 / (1.0 + jnp.exp(-z)))


def _dot_nt(a, b):
    return lax.dot_general(a, b, (((1,), (1,)), ((), ())), preferred_element_type=_f32)


def _proj_kernel(x_ref, pos_ref, freq_ref, w1_ref, qg_ref, wqt_ref, kvg_ref, wkn_ref, wvt_ref,
                 wp_ref, ps_ref,
                 qnt_ref, qrt_ref, kn_ref, kd_ref, vt_ref, ga_ref, yb_ref,
                 carry_ref):
    tm = x_ref.shape[1]
    ts = PROJ_SUBTILE
    si = pl.program_id(1)

    @pl.when(si == 0)
    def _():
        carry_ref[...] = jnp.zeros_like(carry_ref)

    history = carry_ref[...]
    for r in range(0, tm, ts):
        rows = slice(r, r + ts)
        xb = x_ref[0, rows, :].astype(_bf16)

        ang_t = freq_ref[...] * pos_ref[0, :, rows].astype(_f32)
        cos_t, sin_t = jnp.cos(ang_t), jnp.sin(ang_t)
        tab_t = jnp.concatenate([cos_t, cos_t, sin_t, sin_t], axis=0)
        tab = tab_t.T
        qtab_t = tab_t * _SCORE_SCALE

        h_q = _dot(xb, w1_ref[:, _C_Q:_C_KV])
        u = _dot(xb, w1_ref[:, _C_U:_C_GB])
        h_kv = _dot(xb, w1_ref[:, _C_KV:_C_KR])
        h_kr = _dot(xb, w1_ref[:, _C_KR:_C_GA])
        h_ga = _dot(xb, w1_ref[:, _C_GA:_C_U])
        h_gb = _dot(xb, w1_ref[:, _C_GB:_C_END])

        xqn = _rms_norm(h_q, qg_ref[...])
        qt = _dot_nt(wqt_ref[...], xqn.astype(_bf16))
        qnt_ref[0, :, rows] = (qt[:HEADS * NOPE] * _SCORE_SCALE).astype(_bf16)
        for h in range(HEADS):
            lo = HEADS * NOPE + h * LANES
            qrt_ref[0, h * LANES:(h + 1) * LANES, rows] = (
                qt[lo:lo + LANES] * qtab_t).astype(_bf16)

        xkvn = _rms_norm(h_kv, kvg_ref[...]).astype(_bf16)
        kn_ref[0, rows, :] = _dot(xkvn, wkn_ref[...]).astype(_bf16)
        vt_ref[0, :, rows] = _dot_nt(wvt_ref[...], xkvn).astype(_bf16)
        t = h_kr * tab
        kd_ref[0, rows, :] = (t + pltpu.roll(t, ROPE, axis=1)).astype(_bf16)

        ga_ref[0, rows, :] = _silu(h_ga).astype(_bf16)

        ext = jnp.concatenate([history, u], axis=0)
        history = u[ts - MAX_WINDOW:, :]
        sums = []
        acc = ext
        width = 1
        for g, w in enumerate(POOL_WINDOWS):
            while width < w:
                acc = acc + pltpu.roll(acc, width, axis=0)
                width *= 2
            sums.append(acc[MAX_WINDOW:, :POOL_GROUP])
            acc = acc[:, POOL_GROUP:]
        tok = si * tm + r + lax.broadcasted_iota(jnp.int32, (ts, 1), 0)
        pooled = []
        for g, w in enumerate(POOL_WINDOWS):
            inv_cnt = 1.0 / jnp.minimum(tok + 1, w).astype(_f32)
            pooled.append(sums[g] * inv_cnt - u[:, g * POOL_GROUP:(g + 1) * POOL_GROUP])
        mixed = jnp.concatenate(
            [_dot(jnp.concatenate(pooled[0:2], axis=1).astype(_bf16), wp_ref[0]),
             _dot(jnp.concatenate(pooled[2:4], axis=1).astype(_bf16), wp_ref[1])], axis=1)
        yb_ref[0, rows, :] = (mixed * ps_ref[...] * _silu(h_gb)).astype(_bf16)
    carry_ref[...] = history


def _attn_kernel(qstart_ref, kstart_ref, nfull_ref, nmask_ref,
                 qnt_ref, qrt_ref, kn_ref, kd_ref, vt_ref, ga_ref, pq_ref, pk_ref,
                 o_ref,
                 m_ref, alpha_ref, acc_ref, s_ref, p_ref, bias_ref, *, npairs):
    tq, tk = Q_TILE, K_TILE
    nq = o_ref.shape[1] // tq
    b = pl.program_id(0)
    n_full = nfull_ref[b]
    n_all = n_full + nmask_ref[b]

    m_ref[...] = jnp.full(m_ref.shape, _MASKED, _f32)
    acc_ref[...] = jnp.zeros_like(acc_ref)

    def pair(i):
        qs = pl.multiple_of(qstart_ref[b * npairs + i], tq)
        ks = pl.multiple_of(kstart_ref[b * npairs + i], tk)
        return qs, ks

    chunks = [slice(c, c + S_CHUNK) for c in range(0, tk, S_CHUNK)]
    ones_rows = jnp.ones((L_ROWS, tk), _bf16)

    def make_bias(qs, ks):
        pq = pq_ref[0, :, pl.ds(qs, tq)]
        for c in chunks:
            keep = pk_ref[0, pl.ds(ks + c.start, S_CHUNK), :] <= pq
            bias_ref[c, :] = jnp.where(keep, 0.0, _MASKED)

    def scores(h, qs, ks, masked):
        rows = slice(h * LANES, (h + 1) * LANES)
        qc = jnp.concatenate([qnt_ref[0, rows, pl.ds(qs, tq)],
                              qrt_ref[0, rows, pl.ds(qs, tq)]], axis=0)
        kc = jnp.concatenate([kn_ref[0, pl.ds(ks, tk), rows],
                              kd_ref[0, pl.ds(ks, tk), :]], axis=1)
        s = _dot(kc, qc)
        s_ref[h] = s + bias_ref[...] if masked else s

    def softmax(h, qs):
        cols = pl.ds(qs, tq)
        sb = s_ref.at[h]
        mx = sb[chunks[0], :]
        for c in chunks[1:]:
            mx = jnp.maximum(mx, sb[c, :])
        m_old = m_ref[h, :, cols]
        m_new = jnp.maximum(m_old, jnp.max(mx, axis=0, keepdims=True))
        alpha_ref[h] = jnp.exp2(m_old - m_new)
        m_ref[h, :, cols] = m_new
        for c in chunks:
            p_ref[h, c, :] = jnp.exp2(sb[c, :] - m_new).astype(_bf16)

    def pv(h, qs, ks):
        rows = slice(h * LANES, (h + 1) * LANES)
        cols = pl.ds(qs, tq)
        v_ext = jnp.concatenate([vt_ref[0, rows, pl.ds(ks, tk)], ones_rows], axis=0)
        acc_ref[h, :, cols] = alpha_ref[h] * acc_ref[h, :, cols] + _dot(v_ext, p_ref[h])

    def step(i, next_masked):
        qs, ks = pair(i)
        if next_masked is not None:
            qs_n, ks_n = pair(i + 1)
            if next_masked:
                make_bias(qs_n, ks_n)
        for h in range(HEADS):
            pv(h, qs, ks)
            if next_masked is not None:
                scores(h, qs_n, ks_n, next_masked)
            if h + 1 < HEADS:
                softmax(h + 1, qs)
        if next_masked is not None:
            softmax(0, qs_n)

    qs0, ks0 = pair(0)

    @pl.when(n_full > 0)
    def _():
        for h in range(HEADS):
            scores(h, qs0, ks0, False)

    @pl.when(n_full == 0)
    def _():
        make_bias(qs0, ks0)
        for h in range(HEADS):
            scores(h, qs0, ks0, True)

    softmax(0, qs0)

    def step_next_full(i, carry):
        step(i, False)
        return carry

    def step_next_masked(i, carry):
        step(i, True)
        return carry

    first_masked_next = jnp.maximum(n_full - 1, 0)
    lax.fori_loop(0, first_masked_next, step_next_full, 0)
    lax.fori_loop(first_masked_next, n_all - 1, step_next_masked, 0)
    step(n_all - 1, None)

    for qt in range(nq):
        rows = slice(qt * tq, (qt + 1) * tq)
        for h in range(HEADS):
            cols = slice(h * LANES, (h + 1) * LANES)
            inv_l = 1.0 / acc_ref[h, V_DIM:V_DIM + 1, rows]
            o = (acc_ref[h, :V_DIM, rows] * inv_l).T
            o_ref[0, rows, cols] = (o * ga_ref[0, rows, cols].astype(_f32)).astype(_bf16)


def _out_kernel(x_ref, ya_ref, yb_ref, wo_ref, g_ref, b_ref, o_ref, *, alpha):
    for r in range(0, x_ref.shape[1], OUT_SUBTILE):
        rows = slice(r, r + OUT_SUBTILE)
        y = jnp.concatenate([ya_ref[0, rows, :], yb_ref[0, rows, :]], axis=1)
        z = alpha * x_ref[0, rows, :] + _dot(y, wo_ref[...])
        mu = jnp.mean(z, axis=-1, keepdims=True)
        zc = z - mu
        var = jnp.mean(zc * zc, axis=-1, keepdims=True)
        o_ref[0, rows, :] = zc * lax.rsqrt(var + LN_EPS) * g_ref[...] + b_ref[...]


def _rotate_half_cols(w):
    half = w.shape[-1] // 2
    return jnp.concatenate([-w[..., half:], w[..., :half]], axis=-1)


def _prepare_weights(w_in, w_uq, w_ukv, pool_w):
    splits = np.cumsum([Q_RANK, KV_RANK, ROPE, MLA_WIDTH, POOL_WIDTH, POOL_WIDTH])[:-1]
    wq_l, wkv_l, wkr, wga, wu, wgb = jnp.split(w_in, [int(c) for c in splits], axis=1)
    w1 = jnp.concatenate([wq_l, wkv_l, wkr, _rotate_half_cols(wkr), wga, wu, wgb], axis=1)

    wq3 = w_uq.reshape(Q_RANK, HEADS, QK_DIM)
    wq_nope = wq3[:, :, :NOPE].reshape(Q_RANK, HEADS * NOPE)
    wq_rope = wq3[:, :, NOPE:]
    wq_pair = jnp.concatenate([wq_rope, _rotate_half_cols(wq_rope)], axis=-1)
    wqt = jnp.concatenate([wq_nope, wq_pair.reshape(Q_RANK, HEADS * 2 * ROPE)], axis=1).T

    wkv3 = w_ukv.reshape(KV_RANK, HEADS, NOPE + V_DIM)
    wkn = wkv3[:, :, :NOPE].reshape(KV_RANK, HEADS * NOPE)
    wvt = wkv3[:, :, NOPE:].reshape(KV_RANK, HEADS * V_DIM).T

    z = jnp.zeros((POOL_GROUP, POOL_GROUP), pool_w.dtype)
    wp = jnp.stack([jnp.block([[pool_w[0], z], [z, pool_w[1]]]),
                    jnp.block([[pool_w[2], z], [z, pool_w[3]]])])
    return (w1.astype(_bf16), wqt.astype(_bf16), wkn.astype(_bf16), wvt.astype(_bf16),
            wp.astype(_bf16))


def _const_spec(shape):
    return pl.BlockSpec(shape, lambda *_: (0,) * len(shape))


def _layer(x, positions, w1, q_norm_g, wqt, kv_norm_g, wkn, wvt, wp, pool_scale, w_out,
           ln_g, ln_b):
    B, S, D = x.shape
    tm, tq, tk = TOKEN_TILE, Q_TILE, K_TILE
    ns, nq, nk = S // tm, S // tq, S // tk

    half = ROPE // 2
    inv_freq = ROPE_THETA ** (-jnp.arange(half, dtype=_f32) / half)
    freq_col = inv_freq.reshape(half, 1)
    pos_col = positions.reshape(B, S, 1)
    pos_row = positions.reshape(B, 1, S)

    def tok_spec(width):
        return pl.BlockSpec((1, tm, width), lambda b, s: (b, s, 0))

    def feat_spec(width):
        return pl.BlockSpec((1, width, tm), lambda b, s: (b, 0, s))

    act = lambda width: jax.ShapeDtypeStruct((B, S, width), _bf16)
    act_t = lambda width: jax.ShapeDtypeStruct((B, width, S), _bf16)
    qnt, qrt, kn, kd, vt, ga, yb = pl.pallas_call(
        _proj_kernel,
        grid=(B, ns),
        in_specs=[tok_spec(D), feat_spec(1), _const_spec((half, 1)),
                  _const_spec(w1.shape), _const_spec((1, Q_RANK)), _const_spec(wqt.shape),
                  _const_spec((1, KV_RANK)), _const_spec(wkn.shape), _const_spec(wvt.shape),
                  _const_spec(wp.shape), _const_spec((1, POOL_WIDTH))],
        out_specs=[feat_spec(MLA_WIDTH), feat_spec(MLA_WIDTH), tok_spec(MLA_WIDTH),
                   tok_spec(LANES), feat_spec(MLA_WIDTH), tok_spec(MLA_WIDTH),
                   tok_spec(POOL_WIDTH)],
        out_shape=[act_t(MLA_WIDTH), act_t(MLA_WIDTH), act(MLA_WIDTH), act(LANES),
                   act_t(MLA_WIDTH), act(MLA_WIDTH), act(POOL_WIDTH)],
        scratch_shapes=[pltpu.VMEM((MAX_WINDOW, POOL_WIDTH), _f32)],
        compiler_params=pltpu.CompilerParams(
            dimension_semantics=("arbitrary", "arbitrary"),
            vmem_limit_bytes=VMEM_LIMIT_BYTES),
        name="mla_pool_proj",
    )(x, pos_row, freq_col, w1, q_norm_g.reshape(1, -1), wqt, kv_norm_g.reshape(1, -1), wkn,
      wvt, wp, pool_scale.reshape(1, -1))

    pq_blk = positions.reshape(B, nq, tq)
    pk_blk = positions.reshape(B, nk, tk)
    need = pq_blk.max(-1)[:, :, None] >= pk_blk.min(-1)[:, None, :]
    full = pq_blk.min(-1)[:, :, None] >= pk_blk.max(-1)[:, None, :]
    npairs = nq * nk
    rank = jnp.where(need & full, 0, jnp.where(need, 1, 2)).reshape(B, npairs)
    order = jnp.argsort(rank, axis=1, stable=True).astype(jnp.int32)
    q_start = (order // nk) * tq
    k_start = (order % nk) * tk
    n_full = jnp.sum(rank == 0, axis=1, dtype=jnp.int32)
    n_mask = jnp.sum(rank == 1, axis=1, dtype=jnp.int32)

    row_spec = lambda d1, d2: pl.BlockSpec((1, d1, d2), lambda b, *_: (b, 0, 0))
    ya = pl.pallas_call(
        functools.partial(_attn_kernel, npairs=npairs),
        grid_spec=pltpu.PrefetchScalarGridSpec(
            num_scalar_prefetch=4,
            grid=(B,),
            in_specs=[row_spec(MLA_WIDTH, S), row_spec(MLA_WIDTH, S), row_spec(S, MLA_WIDTH),
                      row_spec(S, LANES), row_spec(MLA_WIDTH, S), row_spec(S, MLA_WIDTH),
                      row_spec(1, S), row_spec(S, 1)],
            out_specs=row_spec(S, MLA_WIDTH),
            scratch_shapes=[pltpu.VMEM((HEADS, 1, S), _f32),
                            pltpu.VMEM((HEADS, 1, tq), _f32),
                            pltpu.VMEM((HEADS, V_DIM + L_ROWS, S), _f32),
                            pltpu.VMEM((HEADS, tk, tq), _f32),
                            pltpu.VMEM((HEADS, tk, tq), _bf16),
                            pltpu.VMEM((tk, tq), _f32)]),
        out_shape=act(MLA_WIDTH),
        compiler_params=pltpu.CompilerParams(
            dimension_semantics=("arbitrary",),
            vmem_limit_bytes=VMEM_LIMIT_BYTES),
        name="mla_attention",
    )(q_start.reshape(-1), k_start.reshape(-1), n_full, n_mask,
      qnt, qrt, kn, kd, vt, ga, pos_row, pos_col)

    alpha = (2.0 * DEPTH) ** 0.25
    return pl.pallas_call(
        functools.partial(_out_kernel, alpha=alpha),
        grid=(B, ns),
        in_specs=[tok_spec(D), tok_spec(MLA_WIDTH), tok_spec(POOL_WIDTH),
                  _const_spec(w_out.shape), _const_spec((1, D)), _const_spec((1, D))],
        out_specs=tok_spec(D),
        out_shape=jax.ShapeDtypeStruct((B, S, D), x.dtype),
        compiler_params=pltpu.CompilerParams(
            dimension_semantics=("arbitrary", "arbitrary"),
            vmem_limit_bytes=VMEM_LIMIT_BYTES),
        name="out_proj_layernorm",
    )(x, ya, yb, w_out, ln_g.reshape(1, -1), ln_b.reshape(1, -1))


def kernel(x, positions, w_in, q_norm_g, w_uq, kv_norm_g, w_ukv, pool_w, pool_scale, w_out,
           ln_g, ln_b):
    w1, wqt, wkn, wvt, wp = _prepare_weights(w_in, w_uq, w_ukv, pool_w)
    w_out_b = w_out.astype(_bf16)
    for layer in range(DEPTH):
        x = _layer(x, positions, w1, q_norm_g, wqt, kv_norm_g, wkn, wvt, wp, pool_scale,
                   w_out_b, ln_g[layer], ln_b[layer])
    return x
```

```python
import functools
import math

import jax
import jax.numpy as jnp
import numpy as np
from jax import lax
from jax.experimental import pallas as pl
from jax.experimental.pallas import tpu as pltpu

D_MODEL = 1024
DEPTH = 1
HEADS = 4
NOPE = 128
ROPE = 64
QK_DIM = NOPE + ROPE
V_DIM = 128
Q_RANK = 512
KV_RANK = 256
MLA_WIDTH = HEADS * V_DIM
POOL_WINDOWS = (2, 4, 8, 16)
POOL_GROUP = 128
POOL_WIDTH = len(POOL_WINDOWS) * POOL_GROUP
ROPE_THETA = 10000.0
RMS_EPS = 1e-6
LN_EPS = 1e-5

LANES = 128
MAX_WINDOW = max(POOL_WINDOWS)
VMEM_LIMIT_BYTES = 56 * 1024 * 1024

_C_Q = 0
_C_KV = _C_Q + Q_RANK
_C_KR = _C_KV + KV_RANK
_C_GA = _C_KR + 2 * ROPE
_C_U = _C_GA + MLA_WIDTH
_C_GB = _C_U + POOL_WIDTH
_C_END = _C_GB + POOL_WIDTH

_SCORE_SCALE = (QK_DIM ** -0.5) * math.log2(math.e)
_MASKED = -1e30

TOKEN_TILE = 1024
Q_TILE = 512
K_TILE = 512
S_CHUNK = 32
PROJ_SUBTILE = 256
OUT_SUBTILE = 256
L_ROWS = 16

_bf16 = jnp.bfloat16
_f32 = jnp.float32


def _dot(a, b):
    return jnp.dot(a, b, preferred_element_type=_f32)


def _rms_norm(h, g):
    ms = jnp.mean(h * h, axis=-1, keepdims=True)
    return (h * lax.rsqrt(ms + RMS_EPS)) * g


def _silu(z):
    return z * (1.0 / (1.0 + jnp.exp(-z)))


def _dot_nt(a, b):
    return lax.dot_general(a, b, (((1,), (1,)), ((), ())), preferred_element_type=_f32)


def _proj_kernel(x_ref, pos_ref, freq_ref, w1_ref, qg_ref, wqt_ref, kvg_ref, wkn_ref, wvt_ref,
                 wp_ref, ps_ref,
                 qnt_ref, qrt_ref, kn_ref, kd_ref, vt_ref, ga_ref, yb_ref,
                 carry_ref):
    tm = x_ref.shape[1]
    ts = PROJ_SUBTILE
    si = pl.program_id(1)

    @pl.when(si == 0)
    def _():
        carry_ref[...] = jnp.zeros_like(carry_ref)

    history = carry_ref[...]
    for r in range(0, tm, ts):
        rows = slice(r, r + ts)
        xb = x_ref[0, rows, :].astype(_bf16)

        ang_t = freq_ref[...] * pos_ref[0, :, rows].astype(_f32)
        cos_t, sin_t = jnp.cos(ang_t), jnp.sin(ang_t)
        tab_t = jnp.concatenate([cos_t, cos_t, sin_t, sin_t], axis=0)
        tab = tab_t.T
        qtab_t = tab_t * _SCORE_SCALE

        h_q = _dot(xb, w1_ref[:, _C_Q:_C_KV])
        u = _dot(xb, w1_ref[:, _C_U:_C_GB])
        h_kv = _dot(xb, w1_ref[:, _C_KV:_C_KR])
        h_kr = _dot(xb, w1_ref[:, _C_KR:_C_GA])
        h_ga = _dot(xb, w1_ref[:, _C_GA:_C_U])
        h_gb = _dot(xb, w1_ref[:, _C_GB:_C_END])

        xqn = _rms_norm(h_q, qg_ref[...])
        qt = _dot_nt(wqt_ref[...], xqn.astype(_bf16))
        qnt_ref[0, :, rows] = (qt[:HEADS * NOPE] * _SCORE_SCALE).astype(_bf16)
        for h in range(HEADS):
            lo = HEADS * NOPE + h * LANES
            qrt_ref[0, h * LANES:(h + 1) * LANES, rows] = (
                qt[lo:lo + LANES] * qtab_t).astype(_bf16)

        xkvn = _rms_norm(h_kv, kvg_ref[...]).astype(_bf16)
        kn_ref[0, rows, :] = _dot(xkvn, wkn_ref[...]).astype(_bf16)
        vt_ref[0, :, rows] = _dot_nt(wvt_ref[...], xkvn).astype(_bf16)
        t = h_kr * tab
        kd_ref[0, rows, :] = (t + pltpu.roll(t, ROPE, axis=1)).astype(_bf16)

        ga_ref[0, rows, :] = _silu(h_ga).astype(_bf16)

        ext = jnp.concatenate([history, u], axis=0)
        history = u[ts - MAX_WINDOW:, :]
        sums = []
        acc = ext
        width = 1
        for g, w in enumerate(POOL_WINDOWS):
            while width < w:
                acc = acc + pltpu.roll(acc, width, axis=0)
                width *= 2
            sums.append(acc[MAX_WINDOW:, :POOL_GROUP])
            acc = acc[:, POOL_GROUP:]
        tok = si * tm + r + lax.broadcasted_iota(jnp.int32, (ts, 1), 0)
        pooled = []
        for g, w in enumerate(POOL_WINDOWS):
            inv_cnt = 1.0 / jnp.minimum(tok + 1, w).astype(_f32)
            pooled.append(sums[g] * inv_cnt - u[:, g * POOL_GROUP:(g + 1) * POOL_GROUP])
        mixed = jnp.concatenate(
            [_dot(jnp.concatenate(pooled[0:2], axis=1).astype(_bf16), wp_ref[0]),
             _dot(jnp.concatenate(pooled[2:4], axis=1).astype(_bf16), wp_ref[1])], axis=1)
        yb_ref[0, rows, :] = (mixed * ps_ref[...] * _silu(h_gb)).astype(_bf16)
    carry_ref[...] = history


def _attn_kernel(qstart_ref, kstart_ref, nfull_ref, nmask_ref,
                 qnt_ref, qrt_ref, kn_ref, kd_ref, vt_ref, ga_ref, pos_ref,
                 o_ref,
                 m_ref, alpha_ref, acc_ref, s_ref, smax_ref, p_ref, bias_ref, *, npairs):
    tq, tk = Q_TILE, K_TILE
    nq = o_ref.shape[1] // tq
    b = pl.program_id(0)
    n_full = nfull_ref[b]
    n_all = n_full + nmask_ref[b]

    m_ref[...] = jnp.full(m_ref.shape, _MASKED, _f32)
    acc_ref[...] = jnp.zeros_like(acc_ref)

    def pair(i):
        qs = pl.multiple_of(qstart_ref[b * npairs + i], tq)
        ks = pl.multiple_of(kstart_ref[b * npairs + i], tk)
        return qs, ks

    chunks = [slice(c, c + S_CHUNK) for c in range(0, tk, S_CHUNK)]
    ones_rows = jnp.ones((L_ROWS, tk), _bf16)

    def make_bias(qs, ks):
        for c in range(0, tk, LANES):
            pk_row = pos_ref[0, :, pl.ds(ks + c, LANES)]
            pk_col = jnp.broadcast_to(pk_row, (LANES, LANES)).T
            for j in range(0, tq, LANES):
                keep = pk_col <= pos_ref[0, :, pl.ds(qs + j, LANES)]
                bias_ref[c:c + LANES, j:j + LANES] = jnp.where(keep, 0.0, _MASKED)

    def scores(h, qs, ks, masked):
        rows = slice(h * LANES, (h + 1) * LANES)
        qc = jnp.concatenate([qnt_ref[0, rows, pl.ds(qs, tq)],
                              qrt_ref[0, rows, pl.ds(qs, tq)]], axis=0)
        kc = jnp.concatenate([kn_ref[0, pl.ds(ks, tk), rows],
                              kd_ref[0, pl.ds(ks, tk), :]], axis=1)
        s = _dot(kc, qc)
        if masked:
            s = s + bias_ref[...]
        s_ref[h] = s
        mx = s[chunks[0], :]
        for c in chunks[1:]:
            mx = jnp.maximum(mx, s[c, :])
        smax_ref[h] = mx

    def softmax(h, qs):
        cols = pl.ds(qs, tq)
        sb = s_ref.at[h]
        m_old = m_ref[h, :, cols]
        m_new = jnp.maximum(m_old, jnp.max(smax_ref[h], axis=0, keepdims=True))
        alpha_ref[h] = jnp.exp2(m_old - m_new)
        m_ref[h, :, cols] = m_new
        for c in chunks:
            p_ref[h, c, :] = jnp.exp2(sb[c, :] - m_new).astype(_bf16)

    def pv(h, qs, ks):
        rows = slice(h * LANES, (h + 1) * LANES)
        cols = pl.ds(qs, tq)
        v_ext = jnp.concatenate([vt_ref[0, rows, pl.ds(ks, tk)], ones_rows], axis=0)
        acc_ref[h, :, cols] = alpha_ref[h] * acc_ref[h, :, cols] + _dot(v_ext, p_ref[h])

    def step(i, next_masked):
        qs, ks = pair(i)
        if next_masked is not None:
            qs_n, ks_n = pair(i + 1)
            if next_masked:
                make_bias(qs_n, ks_n)
        for h in range(HEADS):
            pv(h, qs, ks)
            if next_masked is not None:
                scores(h, qs_n, ks_n, next_masked)
            if h + 1 < HEADS:
                softmax(h + 1, qs)
        if next_masked is not None:
            softmax(0, qs_n)

    qs0, ks0 = pair(0)

    @pl.when(n_full > 0)
    def _():
        for h in range(HEADS):
            scores(h, qs0, ks0, False)

    @pl.when(n_full == 0)
    def _():
        make_bias(qs0, ks0)
        for h in range(HEADS):
            scores(h, qs0, ks0, True)

    softmax(0, qs0)

    def step_next_full(i, carry):
        step(i, False)
        return carry

    def step_next_masked(i, carry):
        step(i, True)
        return carry

    first_masked_next = jnp.maximum(n_full - 1, 0)
    lax.fori_loop(0, first_masked_next, step_next_full, 0)
    lax.fori_loop(first_masked_next, n_all - 1, step_next_masked, 0)
    step(n_all - 1, None)

    for qt in range(nq):
        rows = slice(qt * tq, (qt + 1) * tq)
        for h in range(HEADS):
            cols = slice(h * LANES, (h + 1) * LANES)
            inv_l = 1.0 / acc_ref[h, V_DIM:V_DIM + 1, rows]
            o = (acc_ref[h, :V_DIM, rows] * inv_l).T
            o_ref[0, rows, cols] = (o * ga_ref[0, rows, cols].astype(_f32)).astype(_bf16)


def _out_kernel(x_ref, ya_ref, yb_ref, wo_ref, g_ref, b_ref, o_ref, *, alpha):
    for r in range(0, x_ref.shape[1], OUT_SUBTILE):
        rows = slice(r, r + OUT_SUBTILE)
        y = jnp.concatenate([ya_ref[0, rows, :], yb_ref[0, rows, :]], axis=1)
        z = alpha * x_ref[0, rows, :] + _dot(y, wo_ref[...])
        mu = jnp.mean(z, axis=-1, keepdims=True)
        zc = z - mu
        var = jnp.mean(zc * zc, axis=-1, keepdims=True)
        o_ref[0, rows, :] = zc * lax.rsqrt(var + LN_EPS) * g_ref[...] + b_ref[...]


def _rotate_half_cols(w):
    half = w.shape[-1] // 2
    return jnp.concatenate([-w[..., half:], w[..., :half]], axis=-1)


def _prepare_weights(w_in, w_uq, w_ukv, pool_w):
    splits = np.cumsum([Q_RANK, KV_RANK, ROPE, MLA_WIDTH, POOL_WIDTH, POOL_WIDTH])[:-1]
    wq_l, wkv_l, wkr, wga, wu, wgb = jnp.split(w_in, [int(c) for c in splits], axis=1)
    w1 = jnp.concatenate([wq_l, wkv_l, wkr, _rotate_half_cols(wkr), wga, wu, wgb], axis=1)

    wq3 = w_uq.reshape(Q_RANK, HEADS, QK_DIM)
    wq_nope = wq3[:, :, :NOPE].reshape(Q_RANK, HEADS * NOPE)
    wq_rope = wq3[:, :, NOPE:]
    wq_pair = jnp.concatenate([wq_rope, _rotate_half_cols(wq_rope)], axis=-1)
    wqt = jnp.concatenate([wq_nope, wq_pair.reshape(Q_RANK, HEADS * 2 * ROPE)], axis=1).T

    wkv3 = w_ukv.reshape(KV_RANK, HEADS, NOPE + V_DIM)
    wkn = wkv3[:, :, :NOPE].reshape(KV_RANK, HEADS * NOPE)
    wvt = wkv3[:, :, NOPE:].reshape(KV_RANK, HEADS * V_DIM).T

    z = jnp.zeros((POOL_GROUP, POOL_GROUP), pool_w.dtype)
    wp = jnp.stack([jnp.block([[pool_w[0], z], [z, pool_w[1]]]),
                    jnp.block([[pool_w[2], z], [z, pool_w[3]]])])
    return (w1.astype(_bf16), wqt.astype(_bf16), wkn.astype(_bf16), wvt.astype(_bf16),
            wp.astype(_bf16))


def _const_spec(shape):
    return pl.BlockSpec(shape, lambda *_: (0,) * len(shape))


def _layer(x, positions, w1, q_norm_g, wqt, kv_norm_g, wkn, wvt, wp, pool_scale, w_out,
           ln_g, ln_b):
    B, S, D = x.shape
    tm, tq, tk = TOKEN_TILE, Q_TILE, K_TILE
    ns, nq, nk = S // tm, S // tq, S // tk

    half = ROPE // 2
    inv_freq = ROPE_THETA ** (-jnp.arange(half, dtype=_f32) / half)
    freq_col = inv_freq.reshape(half, 1)
    pos_row = positions.reshape(B, 1, S)

    def tok_spec(width):
        return pl.BlockSpec((1, tm, width), lambda b, s: (b, s, 0))

    def feat_spec(width):
        return pl.BlockSpec((1, width, tm), lambda b, s: (b, 0, s))

    act = lambda width: jax.ShapeDtypeStruct((B, S, width), _bf16)
    act_t = lambda width: jax.ShapeDtypeStruct((B, width, S), _bf16)
    qnt, qrt, kn, kd, vt, ga, yb = pl.pallas_call(
        _proj_kernel,
        grid=(B, ns),
        in_specs=[tok_spec(D), feat_spec(1), _const_spec((half, 1)),
                  _const_spec(w1.shape), _const_spec((1, Q_RANK)), _const_spec(wqt.shape),
                  _const_spec((1, KV_RANK)), _const_spec(wkn.shape), _const_spec(wvt.shape),
                  _const_spec(wp.shape), _const_spec((1, POOL_WIDTH))],
        out_specs=[feat_spec(MLA_WIDTH), feat_spec(MLA_WIDTH), tok_spec(MLA_WIDTH),
                   tok_spec(LANES), feat_spec(MLA_WIDTH), tok_spec(MLA_WIDTH),
                   tok_spec(POOL_WIDTH)],
        out_shape=[act_t(MLA_WIDTH), act_t(MLA_WIDTH), act(MLA_WIDTH), act(LANES),
                   act_t(MLA_WIDTH), act(MLA_WIDTH), act(POOL_WIDTH)],
        scratch_shapes=[pltpu.VMEM((MAX_WINDOW, POOL_WIDTH), _f32)],
        compiler_params=pltpu.CompilerParams(
            dimension_semantics=("arbitrary", "arbitrary"),
            vmem_limit_bytes=VMEM_LIMIT_BYTES),
        name="mla_pool_proj",
    )(x, pos_row, freq_col, w1, q_norm_g.reshape(1, -1), wqt, kv_norm_g.reshape(1, -1), wkn,
      wvt, wp, pool_scale.reshape(1, -1))

    pq_blk = positions.reshape(B, nq, tq)
    pk_blk = positions.reshape(B, nk, tk)
    need = pq_blk.max(-1)[:, :, None] >= pk_blk.min(-1)[:, None, :]
    full = pq_blk.min(-1)[:, :, None] >= pk_blk.max(-1)[:, None, :]
    npairs = nq * nk
    rank = jnp.where(need & full, 0, jnp.where(need, 1, 2)).reshape(B, npairs)
    order = jnp.argsort(rank, axis=1, stable=True).astype(jnp.int32)
    q_start = (order // nk) * tq
    k_start = (order % nk) * tk
    n_full = jnp.sum(rank == 0, axis=1, dtype=jnp.int32)
    n_mask = jnp.sum(rank == 1, axis=1, dtype=jnp.int32)

    row_spec = lambda d1, d2: pl.BlockSpec((1, d1, d2), lambda b, *_: (b, 0, 0))
    ya = pl.pallas_call(
        functools.partial(_attn_kernel, npairs=npairs),
        grid_spec=pltpu.PrefetchScalarGridSpec(
            num_scalar_prefetch=4,
            grid=(B,),
            in_specs=[row_spec(MLA_WIDTH, S), row_spec(MLA_WIDTH, S), row_spec(S, MLA_WIDTH),
                      row_spec(S, LANES), row_spec(MLA_WIDTH, S), row_spec(S, MLA_WIDTH),
                      row_spec(1, S)],
            out_specs=row_spec(S, MLA_WIDTH),
            scratch_shapes=[pltpu.VMEM((HEADS, 1, S), _f32),
                            pltpu.VMEM((HEADS, 1, tq), _f32),
                            pltpu.VMEM((HEADS, V_DIM + L_ROWS, S), _f32),
                            pltpu.VMEM((HEADS, tk, tq), _f32),
                            pltpu.VMEM((HEADS, S_CHUNK, tq), _f32),
                            pltpu.VMEM((HEADS, tk, tq), _bf16),
                            pltpu.VMEM((tk, tq), _f32)]),
        out_shape=act(MLA_WIDTH),
        compiler_params=pltpu.CompilerParams(
            dimension_semantics=("arbitrary",),
            vmem_limit_bytes=VMEM_LIMIT_BYTES),
        name="mla_attention",
    )(q_start.reshape(-1), k_start.reshape(-1), n_full, n_mask,
      qnt, qrt, kn, kd, vt, ga, pos_row)

    alpha = (2.0 * DEPTH) ** 0.25
    return pl.pallas_call(
        functools.partial(_out_kernel, alpha=alpha),
        grid=(B, ns),
        in_specs=[tok_spec(D), tok_spec(MLA_WIDTH), tok_spec(POOL_WIDTH),
                  _const_spec(w_out.shape), _const_spec((1, D)), _const_spec((1, D))],
        out_specs=tok_spec(D),
        out_shape=jax.ShapeDtypeStruct((B, S, D), x.dtype),
        compiler_params=pltpu.CompilerParams(
            dimension_semantics=("arbitrary", "arbitrary"),
            vmem_limit_bytes=VMEM_LIMIT_BYTES),
        name="out_proj_layernorm",
    )(x, ya, yb, w_out, ln_g.reshape(1, -1), ln_b.reshape(1, -1))


def kernel(x, positions, w_in, q_norm_g, w_uq, kv_norm_g, w_ukv, pool_w, pool_scale, w_out,
           ln_g, ln_b):
    w1, wqt, wkn, wvt, wp = _prepare_weights(w_in, w_uq, w_ukv, pool_w)
    w_out_b = w_out.astype(_bf16)
    for layer in range(DEPTH):
        x = _layer(x, positions, w1, q_norm_g, wqt, kv_norm_g, wkn, wvt, wp, pool_scale,
                   w_out_b, ln_g[layer], ln_b[layer])
    return x
```

```python
import functools
import math

import jax
import jax.numpy as jnp
import numpy as np
from jax import lax
from jax.experimental import pallas as pl
from jax.experimental.pallas import tpu as pltpu

D_MODEL = 1024
DEPTH = 1
HEADS = 4
NOPE = 128
ROPE = 64
QK_DIM = NOPE + ROPE
V_DIM = 128
Q_RANK = 512
KV_RANK = 256
MLA_WIDTH = HEADS * V_DIM
POOL_WINDOWS = (2, 4, 8, 16)
POOL_GROUP = 128
POOL_WIDTH = len(POOL_WINDOWS) * POOL_GROUP
ROPE_THETA = 10000.0
RMS_EPS = 1e-6
LN_EPS = 1e-5

LANES = 128
MAX_WINDOW = max(POOL_WINDOWS)
VMEM_LIMIT_BYTES = 56 * 1024 * 1024

_C_Q = 0
_C_KV = _C_Q + Q_RANK
_C_KR = _C_KV + KV_RANK
_C_GA = _C_KR + 2 * ROPE
_C_U = _C_GA + MLA_WIDTH
_C_GB = _C_U + POOL_WIDTH
_C_END = _C_GB + POOL_WIDTH

_SCORE_SCALE = (QK_DIM ** -0.5) * math.log2(math.e)
_MASKED = -1e30

TOKEN_TILE = 1024
Q_TILE = 512
K_TILE = 512
STEP_UNROLL = 1
SOFTMAX_LEAD = 1
S_CHUNK = 16
PROJ_SUBTILE = 256
OUT_SUBTILE = 256
L_ROWS = 16

_bf16 = jnp.bfloat16
_f32 = jnp.float32


def _dot(a, b):
    return jnp.dot(a, b, preferred_element_type=_f32)


def _rms_norm(h, g):
    ms = jnp.mean(h * h, axis=-1, keepdims=True)
    return (h * lax.rsqrt(ms + RMS_EPS)) * g


def _silu(z):
    return z * (1.0 / (1.0 + jnp.exp(-z)))


def _dot_nt(a, b):
    return lax.dot_general(a, b, (((1,), (1,)), ((), ())), preferred_element_type=_f32)


def _proj_kernel(x_ref, pos_ref, freq_ref, w1_ref, qg_ref, wqt_ref, kvg_ref, wkn_ref, wvt_ref,
                 wp_ref, ps_ref,
                 qnt_ref, qrt_ref, kn_ref, kd_ref, vt_ref, ga_ref, yb_ref,
                 carry_ref):
    tm = x_ref.shape[1]
    ts = PROJ_SUBTILE
    si = pl.program_id(1)

    @pl.when(si == 0)
    def _():
        carry_ref[...] = jnp.zeros_like(carry_ref)

    history = carry_ref[...]
    for r in range(0, tm, ts):
        rows = slice(r, r + ts)
        xb = x_ref[0, rows, :].astype(_bf16)

        ang_t = freq_ref[...] * pos_ref[0, :, rows].astype(_f32)
        cos_t, sin_t = jnp.cos(ang_t), jnp.sin(ang_t)
        tab_t = jnp.concatenate([cos_t, cos_t, sin_t, sin_t], axis=0)
        tab = tab_t.T
        qcos_t, qsin_t = cos_t * _SCORE_SCALE, sin_t * _SCORE_SCALE

        h_q = _dot(xb, w1_ref[:, _C_Q:_C_KV])
        u = _dot(xb, w1_ref[:, _C_U:_C_GB])
        h_kv = _dot(xb, w1_ref[:, _C_KV:_C_KR])
        h_kr = _dot(xb, w1_ref[:, _C_KR:_C_GA])
        h_ga = _dot(xb, w1_ref[:, _C_GA:_C_U])
        h_gb = _dot(xb, w1_ref[:, _C_GB:_C_END])

        xqn = _rms_norm(h_q, qg_ref[...])
        qt = _dot_nt(wqt_ref[...], xqn.astype(_bf16))
        qnt_ref[0, :, rows] = (qt[:HEADS * NOPE] * _SCORE_SCALE).astype(_bf16)
        half = ROPE // 2
        for h in range(HEADS):
            lo = HEADS * NOPE + h * ROPE
            t1, t2 = qt[lo:lo + half], qt[lo + half:lo + ROPE]
            roped = jnp.concatenate([t1 * qcos_t - t2 * qsin_t,
                                     t1 * qsin_t + t2 * qcos_t], axis=0)
            qrt_ref[0, h * LANES:h * LANES + ROPE, rows] = roped.astype(_bf16)
            qrt_ref[0, h * LANES + ROPE:(h + 1) * LANES, rows] = jnp.zeros((ROPE, ts), _bf16)

        xkvn = _rms_norm(h_kv, kvg_ref[...]).astype(_bf16)
        kn_ref[0, rows, :] = _dot(xkvn, wkn_ref[...]).astype(_bf16)
        vt_ref[0, :, rows] = _dot_nt(wvt_ref[...], xkvn).astype(_bf16)
        t = h_kr * tab
        kd_ref[0, rows, :] = (t + pltpu.roll(t, ROPE, axis=1)).astype(_bf16)

        ga_ref[0, rows, :] = _silu(h_ga).astype(_bf16)

        ext = jnp.concatenate([history, u], axis=0)
        history = u[ts - MAX_WINDOW:, :]
        sums = []
        acc = ext
        width = 1
        for g, w in enumerate(POOL_WINDOWS):
            while width < w:
                acc = acc + pltpu.roll(acc, width, axis=0)
                width *= 2
            sums.append(acc[MAX_WINDOW:, :POOL_GROUP])
            acc = acc[:, POOL_GROUP:]
        tok = si * tm + r + lax.broadcasted_iota(jnp.int32, (ts, 1), 0)
        pooled = []
        for g, w in enumerate(POOL_WINDOWS):
            inv_cnt = 1.0 / jnp.minimum(tok + 1, w).astype(_f32)
            pooled.append(sums[g] * inv_cnt - u[:, g * POOL_GROUP:(g + 1) * POOL_GROUP])
        mixed = jnp.concatenate(
            [_dot(jnp.concatenate(pooled[0:2], axis=1).astype(_bf16), wp_ref[0]),
             _dot(jnp.concatenate(pooled[2:4], axis=1).astype(_bf16), wp_ref[1])], axis=1)
        yb_ref[0, rows, :] = (mixed * ps_ref[...] * _silu(h_gb)).astype(_bf16)
    carry_ref[...] = history


def _attn_kernel(qstart_ref, kstart_ref, nfull_ref, nmask_ref,
                 qnt_ref, qrt_ref, kn_ref, kd_ref, vt_ref, ga_ref, pos_ref,
                 o_ref,
                 m_ref, alpha_ref, acc_ref, bias_ref, *head_refs, npairs):
    tq, tk = Q_TILE, K_TILE
    s_refs, smax_refs, p_refs = (head_refs[i * HEADS:(i + 1) * HEADS] for i in range(3))
    nq = o_ref.shape[1] // tq
    b = pl.program_id(0)
    n_full = nfull_ref[b]
    n_all = n_full + nmask_ref[b]

    m_ref[...] = jnp.full(m_ref.shape, _MASKED, _f32)
    acc_ref[...] = jnp.zeros_like(acc_ref)

    def pair(i):
        qs = pl.multiple_of(qstart_ref[b * npairs + i], tq)
        ks = pl.multiple_of(kstart_ref[b * npairs + i], tk)
        return qs, ks

    chunks = [slice(c, c + S_CHUNK) for c in range(0, tk, S_CHUNK)]
    ones_rows = jnp.ones((L_ROWS, tk), _bf16)

    def make_bias(qs, ks):
        for c in range(0, tk, LANES):
            pk_row = pos_ref[0, :, pl.ds(ks + c, LANES)]
            pk_col = jnp.broadcast_to(pk_row, (LANES, LANES)).T
            for j in range(0, tq, LANES):
                keep = pk_col <= pos_ref[0, :, pl.ds(qs + j, LANES)]
                bias_ref[c:c + LANES, j:j + LANES] = jnp.where(keep, 0.0, _MASKED)

    def scores(h, qs, ks, masked):
        rows = slice(h * LANES, (h + 1) * LANES)
        qc = jnp.concatenate([qnt_ref[0, rows, pl.ds(qs, tq)],
                              qrt_ref[0, rows, pl.ds(qs, tq)]], axis=0)
        kc = jnp.concatenate([kn_ref[0, pl.ds(ks, tk), rows],
                              kd_ref[0, pl.ds(ks, tk), :]], axis=1)
        s = _dot(kc, qc)
        if masked:
            s = s + bias_ref[...]
        s_refs[h][...] = s
        mx = s[chunks[0], :]
        for c in chunks[1:]:
            mx = jnp.maximum(mx, s[c, :])
        smax_refs[h][...] = mx

    def softmax(h, qs):
        cols = pl.ds(qs, tq)
        sb = s_refs[h]
        m_old = m_ref[h, :, cols]
        m_new = jnp.maximum(m_old, jnp.max(smax_refs[h][...], axis=0, keepdims=True))
        alpha_ref[h] = jnp.exp2(m_old - m_new)
        m_ref[h, :, cols] = m_new
        for c in chunks:
            p_refs[h][c, :] = jnp.exp2(sb[c, :] - m_new).astype(_bf16)

    def pv(h, qs, ks):
        rows = slice(h * LANES, (h + 1) * LANES)
        cols = pl.ds(qs, tq)
        v_ext = jnp.concatenate([vt_ref[0, rows, pl.ds(ks, tk)], ones_rows], axis=0)
        acc_ref[h, :, cols] = alpha_ref[h] * acc_ref[h, :, cols] + _dot(v_ext, p_refs[h][...])

    def step(i, next_masked):
        qs, ks = pair(i)
        if next_masked is not None:
            qs_n, ks_n = pair(i + 1)
            if next_masked:
                make_bias(qs_n, ks_n)
        for h in range(HEADS):
            pv(h, qs, ks)
            if next_masked is not None:
                scores(h, qs_n, ks_n, next_masked)
            ahead = h + SOFTMAX_LEAD
            if ahead < HEADS:
                softmax(ahead, qs)
            elif next_masked is not None:
                softmax(ahead - HEADS, qs_n)

    def pipelined_steps(first, count, next_masked):
        def body(j, carry):
            for k in range(STEP_UNROLL):
                step(first + j * STEP_UNROLL + k, next_masked)
            return carry

        groups = count // STEP_UNROLL
        lax.fori_loop(0, groups, body, 0)
        for k in range(STEP_UNROLL - 1):
            @pl.when(count - groups * STEP_UNROLL > k)
            def _():
                step(first + groups * STEP_UNROLL + k, next_masked)

    qs0, ks0 = pair(0)

    @pl.when(n_full > 0)
    def _():
        for h in range(HEADS):
            scores(h, qs0, ks0, False)

    @pl.when(n_full == 0)
    def _():
        make_bias(qs0, ks0)
        for h in range(HEADS):
            scores(h, qs0, ks0, True)

    for h in range(SOFTMAX_LEAD):
        softmax(h, qs0)

    first_masked_next = jnp.maximum(n_full - 1, 0)
    pipelined_steps(0, first_masked_next, False)
    pipelined_steps(first_masked_next, n_all - 1 - first_masked_next, True)
    step(n_all - 1, None)

    for qt in range(nq):
        rows = slice(qt * tq, (qt + 1) * tq)
        for h in range(HEADS):
            cols = slice(h * LANES, (h + 1) * LANES)
            inv_l = 1.0 / acc_ref[h, V_DIM:V_DIM + 1, rows]
            o = (acc_ref[h, :V_DIM, rows] * inv_l).T
            o_ref[0, rows, cols] = (o * ga_ref[0, rows, cols].astype(_f32)).astype(_bf16)


def _out_kernel(x_ref, ya_ref, yb_ref, wo_ref, g_ref, b_ref, o_ref, *, alpha):
    for r in range(0, x_ref.shape[1], OUT_SUBTILE):
        rows = slice(r, r + OUT_SUBTILE)
        y = jnp.concatenate([ya_ref[0, rows, :], yb_ref[0, rows, :]], axis=1)
        z = alpha * x_ref[0, rows, :] + _dot(y, wo_ref[...])
        mu = jnp.mean(z, axis=-1, keepdims=True)
        zc = z - mu
        var = jnp.mean(zc * zc, axis=-1, keepdims=True)
        o_ref[0, rows, :] = zc * lax.rsqrt(var + LN_EPS) * g_ref[...] + b_ref[...]


def _rotate_half_cols(w):
    half = w.shape[-1] // 2
    return jnp.concatenate([-w[..., half:], w[..., :half]], axis=-1)


def _prepare_weights(w_in, w_uq, w_ukv, pool_w):
    splits = np.cumsum([Q_RANK, KV_RANK, ROPE, MLA_WIDTH, POOL_WIDTH, POOL_WIDTH])[:-1]
    wq_l, wkv_l, wkr, wga, wu, wgb = jnp.split(w_in, [int(c) for c in splits], axis=1)
    w1 = jnp.concatenate([wq_l, wkv_l, wkr, _rotate_half_cols(wkr), wga, wu, wgb], axis=1)

    wq3 = w_uq.reshape(Q_RANK, HEADS, QK_DIM)
    wq_nope = wq3[:, :, :NOPE].reshape(Q_RANK, HEADS * NOPE)
    wq_rope = wq3[:, :, NOPE:].reshape(Q_RANK, HEADS * ROPE)
    wqt = jnp.concatenate([wq_nope, wq_rope], axis=1).T

    wkv3 = w_ukv.reshape(KV_RANK, HEADS, NOPE + V_DIM)
    wkn = wkv3[:, :, :NOPE].reshape(KV_RANK, HEADS * NOPE)
    wvt = wkv3[:, :, NOPE:].reshape(KV_RANK, HEADS * V_DIM).T

    z = jnp.zeros((POOL_GROUP, POOL_GROUP), pool_w.dtype)
    wp = jnp.stack([jnp.block([[pool_w[0], z], [z, pool_w[1]]]),
                    jnp.block([[pool_w[2], z], [z, pool_w[3]]])])
    return (w1.astype(_bf16), wqt.astype(_bf16), wkn.astype(_bf16), wvt.astype(_bf16),
            wp.astype(_bf16))


def _const_spec(shape):
    return pl.BlockSpec(shape, lambda *_: (0,) * len(shape))


def _layer(x, positions, w1, q_norm_g, wqt, kv_norm_g, wkn, wvt, wp, pool_scale, w_out,
           ln_g, ln_b):
    B, S, D = x.shape
    tm, tq, tk = TOKEN_TILE, Q_TILE, K_TILE
    ns, nq, nk = S // tm, S // tq, S // tk

    half = ROPE // 2
    inv_freq = ROPE_THETA ** (-jnp.arange(half, dtype=_f32) / half)
    freq_col = inv_freq.reshape(half, 1)
    pos_row = positions.reshape(B, 1, S)

    def tok_spec(width):
        return pl.BlockSpec((1, tm, width), lambda b, s: (b, s, 0))

    def feat_spec(width):
        return pl.BlockSpec((1, width, tm), lambda b, s: (b, 0, s))

    act = lambda width: jax.ShapeDtypeStruct((B, S, width), _bf16)
    act_t = lambda width: jax.ShapeDtypeStruct((B, width, S), _bf16)
    qnt, qrt, kn, kd, vt, ga, yb = pl.pallas_call(
        _proj_kernel,
        grid=(B, ns),
        in_specs=[tok_spec(D), feat_spec(1), _const_spec((half, 1)),
                  _const_spec(w1.shape), _const_spec((1, Q_RANK)), _const_spec(wqt.shape),
                  _const_spec((1, KV_RANK)), _const_spec(wkn.shape), _const_spec(wvt.shape),
                  _const_spec(wp.shape), _const_spec((1, POOL_WIDTH))],
        out_specs=[feat_spec(MLA_WIDTH), feat_spec(MLA_WIDTH), tok_spec(MLA_WIDTH),
                   tok_spec(LANES), feat_spec(MLA_WIDTH), tok_spec(MLA_WIDTH),
                   tok_spec(POOL_WIDTH)],
        out_shape=[act_t(MLA_WIDTH), act_t(MLA_WIDTH), act(MLA_WIDTH), act(LANES),
                   act_t(MLA_WIDTH), act(MLA_WIDTH), act(POOL_WIDTH)],
        scratch_shapes=[pltpu.VMEM((MAX_WINDOW, POOL_WIDTH), _f32)],
        compiler_params=pltpu.CompilerParams(
            dimension_semantics=("arbitrary", "arbitrary"),
            vmem_limit_bytes=VMEM_LIMIT_BYTES),
        name="mla_pool_proj",
    )(x, pos_row, freq_col, w1, q_norm_g.reshape(1, -1), wqt, kv_norm_g.reshape(1, -1), wkn,
      wvt, wp, pool_scale.reshape(1, -1))

    pq_blk = positions.reshape(B, nq, tq)
    pk_blk = positions.reshape(B, nk, tk)
    need = pq_blk.max(-1)[:, :, None] >= pk_blk.min(-1)[:, None, :]
    full = pq_blk.min(-1)[:, :, None] >= pk_blk.max(-1)[:, None, :]
    npairs = nq * nk
    rank = jnp.where(need & full, 0, jnp.where(need, 1, 2)).reshape(B, npairs)
    order = jnp.argsort(rank, axis=1, stable=True).astype(jnp.int32)
    q_start = (order // nk) * tq
    k_start = (order % nk) * tk
    n_full = jnp.sum(rank == 0, axis=1, dtype=jnp.int32)
    n_mask = jnp.sum(rank == 1, axis=1, dtype=jnp.int32)

    row_spec = lambda d1, d2: pl.BlockSpec((1, d1, d2), lambda b, *_: (b, 0, 0))
    ya = pl.pallas_call(
        functools.partial(_attn_kernel, npairs=npairs),
        grid_spec=pltpu.PrefetchScalarGridSpec(
            num_scalar_prefetch=4,
            grid=(B,),
            in_specs=[row_spec(MLA_WIDTH, S), row_spec(MLA_WIDTH, S), row_spec(S, MLA_WIDTH),
                      row_spec(S, LANES), row_spec(MLA_WIDTH, S), row_spec(S, MLA_WIDTH),
                      row_spec(1, S)],
            out_specs=row_spec(S, MLA_WIDTH),
            scratch_shapes=[pltpu.VMEM((HEADS, 1, S), _f32),
                            pltpu.VMEM((HEADS, 1, tq), _f32),
                            pltpu.VMEM((HEADS, V_DIM + L_ROWS, S), _f32),
                            pltpu.VMEM((tk, tq), _f32)]
            + [pltpu.VMEM((tk, tq), _f32) for _ in range(HEADS)]
            + [pltpu.VMEM((S_CHUNK, tq), _f32) for _ in range(HEADS)]
            + [pltpu.VMEM((tk, tq), _bf16) for _ in range(HEADS)]),
        out_shape=act(MLA_WIDTH),
        compiler_params=pltpu.CompilerParams(
            dimension_semantics=("arbitrary",),
            vmem_limit_bytes=VMEM_LIMIT_BYTES),
        name="mla_attention",
    )(q_start.reshape(-1), k_start.reshape(-1), n_full, n_mask,
      qnt, qrt, kn, kd, vt, ga, pos_row)

    alpha = (2.0 * DEPTH) ** 0.25
    return pl.pallas_call(
        functools.partial(_out_kernel, alpha=alpha),
        grid=(B, ns),
        in_specs=[tok_spec(D), tok_spec(MLA_WIDTH), tok_spec(POOL_WIDTH),
                  _const_spec(w_out.shape), _const_spec((1, D)), _const_spec((1, D))],
        out_specs=tok_spec(D),
        out_shape=jax.ShapeDtypeStruct((B, S, D), x.dtype),
        compiler_params=pltpu.CompilerParams(
            dimension_semantics=("arbitrary", "arbitrary"),
            vmem_limit_bytes=VMEM_LIMIT_BYTES),
        name="out_proj_layernorm",
    )(x, ya, yb, w_out, ln_g.reshape(1, -1), ln_b.reshape(1, -1))


def kernel(x, positions, w_in, q_norm_g, w_uq, kv_norm_g, w_ukv, pool_w, pool_scale, w_out,
           ln_g, ln_b):
    w1, wqt, wkn, wvt, wp = _prepare_weights(w_in, w_uq, w_ukv, pool_w)
    w_out_b = w_out.astype(_bf16)
    for layer in range(DEPTH):
        x = _layer(x, positions, w1, q_norm_g, wqt, kv_norm_g, wkn, wvt, wp, pool_scale,
                   w_out_b, ln_g[layer], ln_b[layer])
    return x
```

```python
import functools
import math

import jax
import jax.numpy as jnp
import numpy as np
from jax import lax
from jax.experimental import pallas as pl
from jax.experimental.pallas import tpu as pltpu

D_MODEL = 1024
DEPTH = 1
HEADS = 4
NOPE = 128
ROPE = 64
QK_DIM = NOPE + ROPE
V_DIM = 128
Q_RANK = 512
KV_RANK = 256
MLA_WIDTH = HEADS * V_DIM
POOL_WINDOWS = (2, 4, 8, 16)
POOL_GROUP = 128
POOL_WIDTH = len(POOL_WINDOWS) * POOL_GROUP
ROPE_THETA = 10000.0
RMS_EPS = 1e-6
LN_EPS = 1e-5

LANES = 128
MAX_WINDOW = max(POOL_WINDOWS)
VMEM_LIMIT_BYTES = 56 * 1024 * 1024

_C_Q = 0
_C_KV = _C_Q + Q_RANK
_C_KR = _C_KV + KV_RANK
_C_GA = _C_KR + 2 * ROPE
_C_U = _C_GA + MLA_WIDTH
_C_GB = _C_U + POOL_WIDTH
_C_END = _C_GB + POOL_WIDTH

_SCORE_SCALE = (QK_DIM ** -0.5) * math.log2(math.e)
_MASKED = -1e30

TOKEN_TILE = 1024
OUT_TILE = 2048
Q_TILE = 512
K_TILE = 512
STEP_UNROLL = 1
SOFTMAX_LEAD = 1
S_CHUNK = 16
PROJ_SUBTILE = 256
OUT_SUBTILE = 256
L_ROWS = 16

_bf16 = jnp.bfloat16
_f32 = jnp.float32


def _dot(a, b):
    return jnp.dot(a, b, preferred_element_type=_f32)


def _rms_norm(h, g):
    ms = jnp.mean(h * h, axis=-1, keepdims=True)
    return (h * lax.rsqrt(ms + RMS_EPS)) * g


def _silu(z):
    return z * (1.0 / (1.0 + jnp.exp(-z)))


def _dot_nt(a, b):
    return lax.dot_general(a, b, (((1,), (1,)), ((), ())), preferred_element_type=_f32)


def _proj_kernel(x_ref, pos_ref, freq_ref, w1_ref, qg_ref, wqt_ref, kvg_ref, wkn_ref, wvt_ref,
                 wp_ref, ps_ref,
                 qnt_ref, qrt_ref, kn_ref, kd_ref, vt_ref, ga_ref, yb_ref,
                 carry_ref):
    tm = x_ref.shape[1]
    ts = PROJ_SUBTILE
    si = pl.program_id(1)

    @pl.when(si == 0)
    def _():
        carry_ref[...] = jnp.zeros_like(carry_ref)

    history = carry_ref[...]
    for r in range(0, tm, ts):
        rows = slice(r, r + ts)
        xb = x_ref[0, rows, :].astype(_bf16)

        ang_t = freq_ref[...] * pos_ref[0, :, rows].astype(_f32)
        cos_t, sin_t = jnp.cos(ang_t), jnp.sin(ang_t)
        tab_t = jnp.concatenate([cos_t, cos_t, sin_t, sin_t], axis=0)
        tab = tab_t.T
        qcos_t, qsin_t = cos_t * _SCORE_SCALE, sin_t * _SCORE_SCALE

        h_q = _dot(xb, w1_ref[:, _C_Q:_C_KV])
        u = _dot(xb, w1_ref[:, _C_U:_C_GB])
        h_kv = _dot(xb, w1_ref[:, _C_KV:_C_KR])
        h_kr = _dot(xb, w1_ref[:, _C_KR:_C_GA])
        h_ga = _dot(xb, w1_ref[:, _C_GA:_C_U])
        h_gb = _dot(xb, w1_ref[:, _C_GB:_C_END])

        xqn = _rms_norm(h_q, qg_ref[...])
        qt = _dot_nt(wqt_ref[...], xqn.astype(_bf16))
        qnt_ref[0, :, rows] = (qt[:HEADS * NOPE] * _SCORE_SCALE).astype(_bf16)
        half = ROPE // 2
        for h in range(HEADS):
            lo = HEADS * NOPE + h * ROPE
            t1, t2 = qt[lo:lo + half], qt[lo + half:lo + ROPE]
            roped = jnp.concatenate([t1 * qcos_t - t2 * qsin_t,
                                     t1 * qsin_t + t2 * qcos_t], axis=0)
            qrt_ref[0, h * LANES:h * LANES + ROPE, rows] = roped.astype(_bf16)
            qrt_ref[0, h * LANES + ROPE:(h + 1) * LANES, rows] = jnp.zeros((ROPE, ts), _bf16)

        xkvn = _rms_norm(h_kv, kvg_ref[...]).astype(_bf16)
        kn_ref[0, rows, :] = _dot(xkvn, wkn_ref[...]).astype(_bf16)
        vt_ref[0, :, rows] = _dot_nt(wvt_ref[...], xkvn).astype(_bf16)
        t = h_kr * tab
        kd_ref[0, rows, :] = (t + pltpu.roll(t, ROPE, axis=1)).astype(_bf16)

        ga_ref[0, rows, :] = _silu(h_ga).astype(_bf16)

        ext = jnp.concatenate([history, u], axis=0)
        history = u[ts - MAX_WINDOW:, :]
        sums = []
        acc = ext
        width = 1
        for g, w in enumerate(POOL_WINDOWS):
            while width < w:
                acc = acc + pltpu.roll(acc, width, axis=0)
                width *= 2
            sums.append(acc[MAX_WINDOW:, :POOL_GROUP])
            acc = acc[:, POOL_GROUP:]
        tok = si * tm + r + lax.broadcasted_iota(jnp.int32, (ts, 1), 0)
        pooled = []
        for g, w in enumerate(POOL_WINDOWS):
            inv_cnt = 1.0 / jnp.minimum(tok + 1, w).astype(_f32)
            pooled.append(sums[g] * inv_cnt - u[:, g * POOL_GROUP:(g + 1) * POOL_GROUP])
        mixed = jnp.concatenate(
            [_dot(jnp.concatenate(pooled[0:2], axis=1).astype(_bf16), wp_ref[0]),
             _dot(jnp.concatenate(pooled[2:4], axis=1).astype(_bf16), wp_ref[1])], axis=1)
        yb_ref[0, rows, :] = (mixed * ps_ref[...] * _silu(h_gb)).astype(_bf16)
    carry_ref[...] = history


def _attn_kernel(qstart_ref, kstart_ref, nfull_ref, nmask_ref,
                 qnt_ref, qrt_ref, kn_ref, kd_ref, vt_ref, ga_ref, pos_ref,
                 o_ref,
                 m_ref, alpha_ref, acc_ref, bias_ref, *head_refs, npairs):
    tq, tk = Q_TILE, K_TILE
    s_refs, smax_refs, p_refs = (head_refs[i * HEADS:(i + 1) * HEADS] for i in range(3))
    nq = o_ref.shape[1] // tq
    b = pl.program_id(0)
    n_full = nfull_ref[b]
    n_all = n_full + nmask_ref[b]

    m_ref[...] = jnp.full(m_ref.shape, _MASKED, _f32)
    acc_ref[...] = jnp.zeros_like(acc_ref)

    def pair(i):
        qs = pl.multiple_of(qstart_ref[b * npairs + i], tq)
        ks = pl.multiple_of(kstart_ref[b * npairs + i], tk)
        return qs, ks

    chunks = [slice(c, c + S_CHUNK) for c in range(0, tk, S_CHUNK)]
    ones_rows = jnp.ones((L_ROWS, tk), _bf16)

    def make_bias(qs, ks):
        for c in range(0, tk, LANES):
            pk_row = pos_ref[0, :, pl.ds(ks + c, LANES)]
            pk_col = jnp.broadcast_to(pk_row, (LANES, LANES)).T
            for j in range(0, tq, LANES):
                keep = pk_col <= pos_ref[0, :, pl.ds(qs + j, LANES)]
                bias_ref[c:c + LANES, j:j + LANES] = jnp.where(keep, 0.0, _MASKED)

    def scores(h, qs, ks, masked):
        rows = slice(h * LANES, (h + 1) * LANES)
        qc = jnp.concatenate([qnt_ref[0, rows, pl.ds(qs, tq)],
                              qrt_ref[0, rows, pl.ds(qs, tq)]], axis=0)
        kc = jnp.concatenate([kn_ref[0, pl.ds(ks, tk), rows],
                              kd_ref[0, pl.ds(ks, tk), :]], axis=1)
        s = _dot(kc, qc)
        if masked:
            s = s + bias_ref[...]
        s_refs[h][...] = s
        mx = s[chunks[0], :]
        for c in chunks[1:]:
            mx = jnp.maximum(mx, s[c, :])
        smax_refs[h][...] = mx

    def softmax(h, qs):
        cols = pl.ds(qs, tq)
        sb = s_refs[h]
        m_old = m_ref[h, :, cols]
        m_new = jnp.maximum(m_old, jnp.max(smax_refs[h][...], axis=0, keepdims=True))
        alpha_ref[h] = jnp.exp2(m_old - m_new)
        m_ref[h, :, cols] = m_new
        for c in chunks:
            p_refs[h][c, :] = jnp.exp2(sb[c, :] - m_new).astype(_bf16)

    def pv(h, qs, ks):
        rows = slice(h * LANES, (h + 1) * LANES)
        cols = pl.ds(qs, tq)
        v_ext = jnp.concatenate([vt_ref[0, rows, pl.ds(ks, tk)], ones_rows], axis=0)
        acc_ref[h, :, cols] = alpha_ref[h] * acc_ref[h, :, cols] + _dot(v_ext, p_refs[h][...])

    def step(i, next_masked):
        qs, ks = pair(i)
        if next_masked is not None:
            qs_n, ks_n = pair(i + 1)
            if next_masked:
                make_bias(qs_n, ks_n)
        for h in range(HEADS):
            if next_masked is not None:
                scores(h, qs_n, ks_n, next_masked)
            pv(h, qs, ks)
            ahead = h + SOFTMAX_LEAD
            if ahead < HEADS:
                softmax(ahead, qs)
            elif next_masked is not None:
                softmax(ahead - HEADS, qs_n)

    def pipelined_steps(first, count, next_masked):
        def body(j, carry):
            for k in range(STEP_UNROLL):
                step(first + j * STEP_UNROLL + k, next_masked)
            return carry

        groups = count // STEP_UNROLL
        lax.fori_loop(0, groups, body, 0)
        for k in range(STEP_UNROLL - 1):
            @pl.when(count - groups * STEP_UNROLL > k)
            def _():
                step(first + groups * STEP_UNROLL + k, next_masked)

    qs0, ks0 = pair(0)

    @pl.when(n_full > 0)
    def _():
        for h in range(HEADS):
            scores(h, qs0, ks0, False)

    @pl.when(n_full == 0)
    def _():
        make_bias(qs0, ks0)
        for h in range(HEADS):
            scores(h, qs0, ks0, True)

    for h in range(SOFTMAX_LEAD):
        softmax(h, qs0)

    first_masked_next = jnp.maximum(n_full - 1, 0)
    pipelined_steps(0, first_masked_next, False)
    pipelined_steps(first_masked_next, n_all - 1 - first_masked_next, True)
    step(n_all - 1, None)

    for qt in range(nq):
        rows = slice(qt * tq, (qt + 1) * tq)
        for h in range(HEADS):
            cols = slice(h * LANES, (h + 1) * LANES)
            inv_l = 1.0 / acc_ref[h, V_DIM:V_DIM + 1, rows]
            o = (acc_ref[h, :V_DIM, rows] * inv_l).T
            o_ref[0, rows, cols] = (o * ga_ref[0, rows, cols].astype(_f32)).astype(_bf16)


def _out_kernel(x_ref, ya_ref, yb_ref, wo_ref, g_ref, b_ref, o_ref, *, alpha):
    for r in range(0, x_ref.shape[1], OUT_SUBTILE):
        rows = slice(r, r + OUT_SUBTILE)
        y = jnp.concatenate([ya_ref[0, rows, :], yb_ref[0, rows, :]], axis=1)
        z = alpha * x_ref[0, rows, :] + _dot(y, wo_ref[...])
        mu = jnp.mean(z, axis=-1, keepdims=True)
        zc = z - mu
        var = jnp.mean(zc * zc, axis=-1, keepdims=True)
        o_ref[0, rows, :] = zc * lax.rsqrt(var + LN_EPS) * g_ref[...] + b_ref[...]


def _rotate_half_cols(w):
    half = w.shape[-1] // 2
    return jnp.concatenate([-w[..., half:], w[..., :half]], axis=-1)


def _prepare_weights(w_in, w_uq, w_ukv, pool_w):
    w_in, w_uq, w_ukv, pool_w = (w.astype(_bf16) for w in (w_in, w_uq, w_ukv, pool_w))
    splits = np.cumsum([Q_RANK, KV_RANK, ROPE, MLA_WIDTH, POOL_WIDTH, POOL_WIDTH])[:-1]
    wq_l, wkv_l, wkr, wga, wu, wgb = jnp.split(w_in, [int(c) for c in splits], axis=1)
    w1 = jnp.concatenate([wq_l, wkv_l, wkr, _rotate_half_cols(wkr), wga, wu, wgb], axis=1)

    wq3 = w_uq.reshape(Q_RANK, HEADS, QK_DIM)
    wq_nope = wq3[:, :, :NOPE].reshape(Q_RANK, HEADS * NOPE)
    wq_rope = wq3[:, :, NOPE:].reshape(Q_RANK, HEADS * ROPE)
    wqt = jnp.concatenate([wq_nope, wq_rope], axis=1).T

    wkv3 = w_ukv.reshape(KV_RANK, HEADS, NOPE + V_DIM)
    wkn = wkv3[:, :, :NOPE].reshape(KV_RANK, HEADS * NOPE)
    wvt = wkv3[:, :, NOPE:].reshape(KV_RANK, HEADS * V_DIM).T

    z = jnp.zeros((POOL_GROUP, POOL_GROUP), pool_w.dtype)
    wp = jnp.stack([jnp.block([[pool_w[0], z], [z, pool_w[1]]]),
                    jnp.block([[pool_w[2], z], [z, pool_w[3]]])])
    return w1, wqt, wkn, wvt, wp


def _const_spec(shape):
    return pl.BlockSpec(shape, lambda *_: (0,) * len(shape))


def _layer(x, positions, w1, q_norm_g, wqt, kv_norm_g, wkn, wvt, wp, pool_scale, w_out,
           ln_g, ln_b):
    B, S, D = x.shape
    tm, tq, tk = TOKEN_TILE, Q_TILE, K_TILE
    ns, nq, nk = S // tm, S // tq, S // tk

    half = ROPE // 2
    inv_freq = ROPE_THETA ** (-jnp.arange(half, dtype=_f32) / half)
    freq_col = inv_freq.reshape(half, 1)
    pos_row = positions.reshape(B, 1, S)

    def tok_spec(width):
        return pl.BlockSpec((1, tm, width), lambda b, s: (b, s, 0))

    def feat_spec(width):
        return pl.BlockSpec((1, width, tm), lambda b, s: (b, 0, s))

    act = lambda width: jax.ShapeDtypeStruct((B, S, width), _bf16)
    act_t = lambda width: jax.ShapeDtypeStruct((B, width, S), _bf16)
    qnt, qrt, kn, kd, vt, ga, yb = pl.pallas_call(
        _proj_kernel,
        grid=(B, ns),
        in_specs=[tok_spec(D), feat_spec(1), _const_spec((half, 1)),
                  _const_spec(w1.shape), _const_spec((1, Q_RANK)), _const_spec(wqt.shape),
                  _const_spec((1, KV_RANK)), _const_spec(wkn.shape), _const_spec(wvt.shape),
                  _const_spec(wp.shape), _const_spec((1, POOL_WIDTH))],
        out_specs=[feat_spec(MLA_WIDTH), feat_spec(MLA_WIDTH), tok_spec(MLA_WIDTH),
                   tok_spec(LANES), feat_spec(MLA_WIDTH), tok_spec(MLA_WIDTH),
                   tok_spec(POOL_WIDTH)],
        out_shape=[act_t(MLA_WIDTH), act_t(MLA_WIDTH), act(MLA_WIDTH), act(LANES),
                   act_t(MLA_WIDTH), act(MLA_WIDTH), act(POOL_WIDTH)],
        scratch_shapes=[pltpu.VMEM((MAX_WINDOW, POOL_WIDTH), _f32)],
        compiler_params=pltpu.CompilerParams(
            dimension_semantics=("arbitrary", "arbitrary"),
            vmem_limit_bytes=VMEM_LIMIT_BYTES),
        name="mla_pool_proj",
    )(x, pos_row, freq_col, w1, q_norm_g.reshape(1, -1), wqt, kv_norm_g.reshape(1, -1), wkn,
      wvt, wp, pool_scale.reshape(1, -1))

    pq_blk = positions.reshape(B, nq, tq)
    pk_blk = positions.reshape(B, nk, tk)
    need = pq_blk.max(-1)[:, :, None] >= pk_blk.min(-1)[:, None, :]
    full = pq_blk.min(-1)[:, :, None] >= pk_blk.max(-1)[:, None, :]
    npairs = nq * nk
    rank = jnp.where(need & full, 0, jnp.where(need, 1, 2)).reshape(B, npairs)
    order = jnp.argsort(rank, axis=1, stable=True).astype(jnp.int32)
    q_start = (order // nk) * tq
    k_start = (order % nk) * tk
    n_full = jnp.sum(rank == 0, axis=1, dtype=jnp.int32)
    n_mask = jnp.sum(rank == 1, axis=1, dtype=jnp.int32)

    row_spec = lambda d1, d2: pl.BlockSpec((1, d1, d2), lambda b, *_: (b, 0, 0))
    ya = pl.pallas_call(
        functools.partial(_attn_kernel, npairs=npairs),
        grid_spec=pltpu.PrefetchScalarGridSpec(
            num_scalar_prefetch=4,
            grid=(B,),
            in_specs=[row_spec(MLA_WIDTH, S), row_spec(MLA_WIDTH, S), row_spec(S, MLA_WIDTH),
                      row_spec(S, LANES), row_spec(MLA_WIDTH, S), row_spec(S, MLA_WIDTH),
                      row_spec(1, S)],
            out_specs=row_spec(S, MLA_WIDTH),
            scratch_shapes=[pltpu.VMEM((HEADS, 1, S), _f32),
                            pltpu.VMEM((HEADS, 1, tq), _f32),
                            pltpu.VMEM((HEADS, V_DIM + L_ROWS, S), _f32),
                            pltpu.VMEM((tk, tq), _f32)]
            + [pltpu.VMEM((tk, tq), _f32) for _ in range(HEADS)]
            + [pltpu.VMEM((S_CHUNK, tq), _f32) for _ in range(HEADS)]
            + [pltpu.VMEM((tk, tq), _bf16) for _ in range(HEADS)]),
        out_shape=act(MLA_WIDTH),
        compiler_params=pltpu.CompilerParams(
            dimension_semantics=("arbitrary",),
            vmem_limit_bytes=VMEM_LIMIT_BYTES),
        name="mla_attention",
    )(q_start.reshape(-1), k_start.reshape(-1), n_full, n_mask,
      qnt, qrt, kn, kd, vt, ga, pos_row)

    alpha = (2.0 * DEPTH) ** 0.25
    to = OUT_TILE

    def out_spec(width):
        return pl.BlockSpec((1, to, width), lambda b, s: (b, s, 0))

    return pl.pallas_call(
        functools.partial(_out_kernel, alpha=alpha),
        grid=(B, S // to),
        in_specs=[out_spec(D), out_spec(MLA_WIDTH), out_spec(POOL_WIDTH),
                  _const_spec(w_out.shape), _const_spec((1, D)), _const_spec((1, D))],
        out_specs=out_spec(D),
        out_shape=jax.ShapeDtypeStruct((B, S, D), x.dtype),
        compiler_params=pltpu.CompilerParams(
            dimension_semantics=("arbitrary", "arbitrary"),
            vmem_limit_bytes=VMEM_LIMIT_BYTES),
        name="out_proj_layernorm",
    )(x, ya, yb, w_out, ln_g.reshape(1, -1), ln_b.reshape(1, -1))


def kernel(x, positions, w_in, q_norm_g, w_uq, kv_norm_g, w_ukv, pool_w, pool_scale, w_out,
           ln_g, ln_b):
    w1, wqt, wkn, wvt, wp = _prepare_weights(w_in, w_uq, w_ukv, pool_w)
    w_out_b = w_out.astype(_bf16)
    for layer in range(DEPTH):
        x = _layer(x, positions, w1, q_norm_g, wqt, kv_norm_g, wkn, wvt, wp, pool_scale,
                   w_out_b, ln_g[layer], ln_b[layer])
    return x
```

```python
import functools
import math

import jax
import jax.numpy as jnp
import numpy as np
from jax import lax
from jax.experimental import pallas as pl
from jax.experimental.pallas import tpu as pltpu

D_MODEL = 1024
DEPTH = 1
HEADS = 4
NOPE = 128
ROPE = 64
QK_DIM = NOPE + ROPE
V_DIM = 128
Q_RANK = 512
KV_RANK = 256
MLA_WIDTH = HEADS * V_DIM
POOL_WINDOWS = (2, 4, 8, 16)
POOL_GROUP = 128
POOL_WIDTH = len(POOL_WINDOWS) * POOL_GROUP
ROPE_THETA = 10000.0
RMS_EPS = 1e-6
LN_EPS = 1e-5

LANES = 128
MAX_WINDOW = max(POOL_WINDOWS)
VMEM_LIMIT_BYTES = 56 * 1024 * 1024

_C_Q = 0
_C_KV = _C_Q + Q_RANK
_C_KR = _C_KV + KV_RANK
_C_GA = _C_KR + 2 * ROPE
_C_U = _C_GA + MLA_WIDTH
_C_GB = _C_U + POOL_WIDTH
_C_END = _C_GB + POOL_WIDTH

_SCORE_SCALE = (QK_DIM ** -0.5) * math.log2(math.e)
_MASKED = -1e30

TOKEN_TILE = 1024
OUT_TILE = 2048
Q_TILE = 512
K_TILE = 512
STEP_UNROLL = 1
SOFTMAX_LEAD = 1
S_CHUNK = 16
PROJ_SUBTILE = 512
OUT_SUBTILE = 256
L_ROWS = 16

_bf16 = jnp.bfloat16
_f32 = jnp.float32


def _dot(a, b):
    return jnp.dot(a, b, preferred_element_type=_f32)


def _rms_norm(h, g):
    ms = jnp.mean(h * h, axis=-1, keepdims=True)
    return (h * lax.rsqrt(ms + RMS_EPS)) * g


def _silu(z):
    return z * (1.0 / (1.0 + jnp.exp(-z)))


def _dot_nt(a, b):
    return lax.dot_general(a, b, (((1,), (1,)), ((), ())), preferred_element_type=_f32)


def _proj_kernel(x_ref, pos_ref, freq_ref, w1_ref, qg_ref, wqt_ref, kvg_ref, wkn_ref, wvt_ref,
                 wp_ref, ps_ref,
                 qnt_ref, qrt_ref, kn_ref, kd_ref, vt_ref, ga_ref, yb_ref,
                 carry_ref):
    tm = x_ref.shape[1]
    ts = PROJ_SUBTILE
    si = pl.program_id(1)

    @pl.when(si == 0)
    def _():
        carry_ref[...] = jnp.zeros_like(carry_ref)

    def front(r):
        rows = slice(r, r + ts)
        xb = x_ref[0, rows, :].astype(_bf16)

        ang_t = freq_ref[...] * pos_ref[0, :, rows].astype(_f32)
        cos_t, sin_t = jnp.cos(ang_t), jnp.sin(ang_t)
        tab_t = jnp.concatenate([cos_t, cos_t, sin_t, sin_t], axis=0)
        tab = tab_t.T
        qcos_t, qsin_t = cos_t * _SCORE_SCALE, sin_t * _SCORE_SCALE

        h_q = _dot(xb, w1_ref[:, _C_Q:_C_KV])
        u = _dot(xb, w1_ref[:, _C_U:_C_GB])
        h_kv = _dot(xb, w1_ref[:, _C_KV:_C_KR])
        h_kr = _dot(xb, w1_ref[:, _C_KR:_C_GA])
        h_ga = _dot(xb, w1_ref[:, _C_GA:_C_U])
        h_gb = _dot(xb, w1_ref[:, _C_GB:_C_END])
        return h_q, u, h_kv, h_kr, h_ga, h_gb, tab, qcos_t, qsin_t

    def back(r, products, history):
        rows = slice(r, r + ts)
        h_q, u, h_kv, h_kr, h_ga, h_gb, tab, qcos_t, qsin_t = products

        xqn = _rms_norm(h_q, qg_ref[...])
        qt = _dot_nt(wqt_ref[...], xqn.astype(_bf16))
        qnt_ref[0, :, rows] = (qt[:HEADS * NOPE] * _SCORE_SCALE).astype(_bf16)
        half = ROPE // 2
        for h in range(HEADS):
            lo = HEADS * NOPE + h * ROPE
            t1, t2 = qt[lo:lo + half], qt[lo + half:lo + ROPE]
            roped = jnp.concatenate([t1 * qcos_t - t2 * qsin_t,
                                     t1 * qsin_t + t2 * qcos_t], axis=0)
            qrt_ref[0, h * LANES:h * LANES + ROPE, rows] = roped.astype(_bf16)
            qrt_ref[0, h * LANES + ROPE:(h + 1) * LANES, rows] = jnp.zeros((ROPE, ts), _bf16)

        xkvn = _rms_norm(h_kv, kvg_ref[...]).astype(_bf16)
        kn_ref[0, rows, :] = _dot(xkvn, wkn_ref[...]).astype(_bf16)
        vt_ref[0, :, rows] = _dot_nt(wvt_ref[...], xkvn).astype(_bf16)
        t = h_kr * tab
        kd_ref[0, rows, :] = (t + pltpu.roll(t, ROPE, axis=1)).astype(_bf16)

        ga_ref[0, rows, :] = _silu(h_ga).astype(_bf16)

        ext = jnp.concatenate([history, u], axis=0)
        sums = []
        acc = ext
        width = 1
        for g, w in enumerate(POOL_WINDOWS):
            while width < w:
                acc = acc + pltpu.roll(acc, width, axis=0)
                width *= 2
            sums.append(acc[MAX_WINDOW:, :POOL_GROUP])
            acc = acc[:, POOL_GROUP:]
        tok = si * tm + r + lax.broadcasted_iota(jnp.int32, (ts, 1), 0)
        pooled = []
        for g, w in enumerate(POOL_WINDOWS):
            inv_cnt = 1.0 / jnp.minimum(tok + 1, w).astype(_f32)
            pooled.append(sums[g] * inv_cnt - u[:, g * POOL_GROUP:(g + 1) * POOL_GROUP])
        mixed = jnp.concatenate(
            [_dot(jnp.concatenate(pooled[0:2], axis=1).astype(_bf16), wp_ref[0]),
             _dot(jnp.concatenate(pooled[2:4], axis=1).astype(_bf16), wp_ref[1])], axis=1)
        yb_ref[0, rows, :] = (mixed * ps_ref[...] * _silu(h_gb)).astype(_bf16)
        return u[ts - MAX_WINDOW:, :]

    starts = list(range(0, tm, ts))
    history = carry_ref[...]
    products = front(starts[0])
    for i, r in enumerate(starts):
        upcoming = front(starts[i + 1]) if i + 1 < len(starts) else None
        history = back(r, products, history)
        products = upcoming
    carry_ref[...] = history


def _attn_kernel(qstart_ref, kstart_ref, nfull_ref, nmask_ref,
                 qnt_ref, qrt_ref, kn_ref, kd_ref, vt_ref, ga_ref, pos_ref,
                 o_ref,
                 m_ref, alpha_ref, acc_ref, bias_ref, *head_refs, npairs):
    tq, tk = Q_TILE, K_TILE
    s_refs, smax_refs, p_refs = (head_refs[i * HEADS:(i + 1) * HEADS] for i in range(3))
    nq = o_ref.shape[1] // tq
    b = pl.program_id(0)
    n_full = nfull_ref[b]
    n_all = n_full + nmask_ref[b]

    m_ref[...] = jnp.full(m_ref.shape, _MASKED, _f32)
    acc_ref[...] = jnp.zeros_like(acc_ref)

    def pair(i):
        qs = pl.multiple_of(qstart_ref[b * npairs + i], tq)
        ks = pl.multiple_of(kstart_ref[b * npairs + i], tk)
        return qs, ks

    chunks = [slice(c, c + S_CHUNK) for c in range(0, tk, S_CHUNK)]
    ones_rows = jnp.ones((L_ROWS, tk), _bf16)

    def make_bias(qs, ks):
        for c in range(0, tk, LANES):
            pk_row = pos_ref[0, :, pl.ds(ks + c, LANES)]
            pk_col = jnp.broadcast_to(pk_row, (LANES, LANES)).T
            for j in range(0, tq, LANES):
                keep = pk_col <= pos_ref[0, :, pl.ds(qs + j, LANES)]
                bias_ref[c:c + LANES, j:j + LANES] = jnp.where(keep, 0.0, _MASKED)

    def scores(h, qs, ks, masked):
        rows = slice(h * LANES, (h + 1) * LANES)
        qc = jnp.concatenate([qnt_ref[0, rows, pl.ds(qs, tq)],
                              qrt_ref[0, rows, pl.ds(qs, tq)]], axis=0)
        kc = jnp.concatenate([kn_ref[0, pl.ds(ks, tk), rows],
                              kd_ref[0, pl.ds(ks, tk), :]], axis=1)
        s = _dot(kc, qc)
        if masked:
            s = s + bias_ref[...]
        s_refs[h][...] = s
        mx = s[chunks[0], :]
        for c in chunks[1:]:
            mx = jnp.maximum(mx, s[c, :])
        smax_refs[h][...] = mx

    def softmax(h, qs):
        cols = pl.ds(qs, tq)
        sb = s_refs[h]
        m_old = m_ref[h, :, cols]
        m_new = jnp.maximum(m_old, jnp.max(smax_refs[h][...], axis=0, keepdims=True))
        alpha_ref[h] = jnp.exp2(m_old - m_new)
        m_ref[h, :, cols] = m_new
        for c in chunks:
            p_refs[h][c, :] = jnp.exp2(sb[c, :] - m_new).astype(_bf16)

    def pv(h, qs, ks):
        rows = slice(h * LANES, (h + 1) * LANES)
        cols = pl.ds(qs, tq)
        v_ext = jnp.concatenate([vt_ref[0, rows, pl.ds(ks, tk)], ones_rows], axis=0)
        acc_ref[h, :, cols] = alpha_ref[h] * acc_ref[h, :, cols] + _dot(v_ext, p_refs[h][...])

    def step(i, next_masked):
        qs, ks = pair(i)
        if next_masked is not None:
            qs_n, ks_n = pair(i + 1)
            if next_masked:
                make_bias(qs_n, ks_n)
        for h in range(HEADS):
            if next_masked is not None:
                scores(h, qs_n, ks_n, next_masked)
            pv(h, qs, ks)
            ahead = h + SOFTMAX_LEAD
            if ahead < HEADS:
                softmax(ahead, qs)
            elif next_masked is not None:
                softmax(ahead - HEADS, qs_n)

    def pipelined_steps(first, count, next_masked):
        def body(j, carry):
            for k in range(STEP_UNROLL):
                step(first + j * STEP_UNROLL + k, next_masked)
            return carry

        groups = count // STEP_UNROLL
        lax.fori_loop(0, groups, body, 0)
        for k in range(STEP_UNROLL - 1):
            @pl.when(count - groups * STEP_UNROLL > k)
            def _():
                step(first + groups * STEP_UNROLL + k, next_masked)

    qs0, ks0 = pair(0)

    @pl.when(n_full > 0)
    def _():
        for h in range(HEADS):
            scores(h, qs0, ks0, False)

    @pl.when(n_full == 0)
    def _():
        make_bias(qs0, ks0)
        for h in range(HEADS):
            scores(h, qs0, ks0, True)

    for h in range(SOFTMAX_LEAD):
        softmax(h, qs0)

    first_masked_next = jnp.maximum(n_full - 1, 0)
    pipelined_steps(0, first_masked_next, False)
    pipelined_steps(first_masked_next, n_all - 1 - first_masked_next, True)
    step(n_all - 1, None)

    for qt in range(nq):
        rows = slice(qt * tq, (qt + 1) * tq)
        for h in range(HEADS):
            cols = slice(h * LANES, (h + 1) * LANES)
            inv_l = 1.0 / acc_ref[h, V_DIM:V_DIM + 1, rows]
            o = (acc_ref[h, :V_DIM, rows] * inv_l).T
            o_ref[0, rows, cols] = (o * ga_ref[0, rows, cols].astype(_f32)).astype(_bf16)


def _out_kernel(x_ref, ya_ref, yb_ref, wo_ref, g_ref, b_ref, o_ref, *, alpha):
    for r in range(0, x_ref.shape[1], OUT_SUBTILE):
        rows = slice(r, r + OUT_SUBTILE)
        y = jnp.concatenate([ya_ref[0, rows, :], yb_ref[0, rows, :]], axis=1)
        z = alpha * x_ref[0, rows, :] + _dot(y, wo_ref[...])
        mu = jnp.mean(z, axis=-1, keepdims=True)
        zc = z - mu
        var = jnp.mean(zc * zc, axis=-1, keepdims=True)
        o_ref[0, rows, :] = zc * lax.rsqrt(var + LN_EPS) * g_ref[...] + b_ref[...]


def _rotate_half_cols(w):
    half = w.shape[-1] // 2
    return jnp.concatenate([-w[..., half:], w[..., :half]], axis=-1)


def _prepare_weights(w_in, w_uq, w_ukv, pool_w):
    w_in, w_uq, w_ukv, pool_w = (w.astype(_bf16) for w in (w_in, w_uq, w_ukv, pool_w))
    splits = np.cumsum([Q_RANK, KV_RANK, ROPE, MLA_WIDTH, POOL_WIDTH, POOL_WIDTH])[:-1]
    wq_l, wkv_l, wkr, wga, wu, wgb = jnp.split(w_in, [int(c) for c in splits], axis=1)
    w1 = jnp.concatenate([wq_l, wkv_l, wkr, _rotate_half_cols(wkr), wga, wu, wgb], axis=1)

    wq3 = w_uq.reshape(Q_RANK, HEADS, QK_DIM)
    wq_nope = wq3[:, :, :NOPE].reshape(Q_RANK, HEADS * NOPE)
    wq_rope = wq3[:, :, NOPE:].reshape(Q_RANK, HEADS * ROPE)
    wqt = jnp.concatenate([wq_nope, wq_rope], axis=1).T

    wkv3 = w_ukv.reshape(KV_RANK, HEADS, NOPE + V_DIM)
    wkn = wkv3[:, :, :NOPE].reshape(KV_RANK, HEADS * NOPE)
    wvt = wkv3[:, :, NOPE:].reshape(KV_RANK, HEADS * V_DIM).T

    z = jnp.zeros((POOL_GROUP, POOL_GROUP), pool_w.dtype)
    wp = jnp.stack([jnp.block([[pool_w[0], z], [z, pool_w[1]]]),
                    jnp.block([[pool_w[2], z], [z, pool_w[3]]])])
    return w1, wqt, wkn, wvt, wp


def _const_spec(shape):
    return pl.BlockSpec(shape, lambda *_: (0,) * len(shape), pipeline_mode=pl.Buffered(1))


def _layer(x, positions, w1, q_norm_g, wqt, kv_norm_g, wkn, wvt, wp, pool_scale, w_out,
           ln_g, ln_b):
    B, S, D = x.shape
    tm, tq, tk = TOKEN_TILE, Q_TILE, K_TILE
    ns, nq, nk = S // tm, S // tq, S // tk

    half = ROPE // 2
    inv_freq = ROPE_THETA ** (-jnp.arange(half, dtype=_f32) / half)
    freq_col = inv_freq.reshape(half, 1)
    pos_row = positions.reshape(B, 1, S)

    def tok_spec(width):
        return pl.BlockSpec((1, tm, width), lambda b, s: (b, s, 0))

    def feat_spec(width):
        return pl.BlockSpec((1, width, tm), lambda b, s: (b, 0, s))

    act = lambda width: jax.ShapeDtypeStruct((B, S, width), _bf16)
    act_t = lambda width: jax.ShapeDtypeStruct((B, width, S), _bf16)
    qnt, qrt, kn, kd, vt, ga, yb = pl.pallas_call(
        _proj_kernel,
        grid=(B, ns),
        in_specs=[tok_spec(D), feat_spec(1), _const_spec((half, 1)),
                  _const_spec(w1.shape), _const_spec((1, Q_RANK)), _const_spec(wqt.shape),
                  _const_spec((1, KV_RANK)), _const_spec(wkn.shape), _const_spec(wvt.shape),
                  _const_spec(wp.shape), _const_spec((1, POOL_WIDTH))],
        out_specs=[feat_spec(MLA_WIDTH), feat_spec(MLA_WIDTH), tok_spec(MLA_WIDTH),
                   tok_spec(LANES), feat_spec(MLA_WIDTH), tok_spec(MLA_WIDTH),
                   tok_spec(POOL_WIDTH)],
        out_shape=[act_t(MLA_WIDTH), act_t(MLA_WIDTH), act(MLA_WIDTH), act(LANES),
                   act_t(MLA_WIDTH), act(MLA_WIDTH), act(POOL_WIDTH)],
        scratch_shapes=[pltpu.VMEM((MAX_WINDOW, POOL_WIDTH), _f32)],
        compiler_params=pltpu.CompilerParams(
            dimension_semantics=("arbitrary", "arbitrary"),
            vmem_limit_bytes=VMEM_LIMIT_BYTES),
        name="mla_pool_proj",
    )(x, pos_row, freq_col, w1, q_norm_g.reshape(1, -1), wqt, kv_norm_g.reshape(1, -1), wkn,
      wvt, wp, pool_scale.reshape(1, -1))

    pq_blk = positions.reshape(B, nq, tq)
    pk_blk = positions.reshape(B, nk, tk)
    need = pq_blk.max(-1)[:, :, None] >= pk_blk.min(-1)[:, None, :]
    full = pq_blk.min(-1)[:, :, None] >= pk_blk.max(-1)[:, None, :]
    npairs = nq * nk
    rank = jnp.where(need & full, 0, jnp.where(need, 1, 2)).reshape(B, npairs)
    order = jnp.argsort(rank, axis=1, stable=True).astype(jnp.int32)
    q_start = (order // nk) * tq
    k_start = (order % nk) * tk
    n_full = jnp.sum(rank == 0, axis=1, dtype=jnp.int32)
    n_mask = jnp.sum(rank == 1, axis=1, dtype=jnp.int32)

    row_spec = lambda d1, d2: pl.BlockSpec((1, d1, d2), lambda b, *_: (b, 0, 0))
    ya = pl.pallas_call(
        functools.partial(_attn_kernel, npairs=npairs),
        grid_spec=pltpu.PrefetchScalarGridSpec(
            num_scalar_prefetch=4,
            grid=(B,),
            in_specs=[row_spec(MLA_WIDTH, S), row_spec(MLA_WIDTH, S), row_spec(S, MLA_WIDTH),
                      row_spec(S, LANES), row_spec(MLA_WIDTH, S), row_spec(S, MLA_WIDTH),
                      row_spec(1, S)],
            out_specs=row_spec(S, MLA_WIDTH),
            scratch_shapes=[pltpu.VMEM((HEADS, 1, S), _f32),
                            pltpu.VMEM((HEADS, 1, tq), _f32),
                            pltpu.VMEM((HEADS, V_DIM + L_ROWS, S), _f32),
                            pltpu.VMEM((tk, tq), _f32)]
            + [pltpu.VMEM((tk, tq), _f32) for _ in range(HEADS)]
            + [pltpu.VMEM((S_CHUNK, tq), _f32) for _ in range(HEADS)]
            + [pltpu.VMEM((tk, tq), _bf16) for _ in range(HEADS)]),
        out_shape=act(MLA_WIDTH),
        compiler_params=pltpu.CompilerParams(
            dimension_semantics=("arbitrary",),
            vmem_limit_bytes=VMEM_LIMIT_BYTES),
        name="mla_attention",
    )(q_start.reshape(-1), k_start.reshape(-1), n_full, n_mask,
      qnt, qrt, kn, kd, vt, ga, pos_row)

    alpha = (2.0 * DEPTH) ** 0.25
    to = OUT_TILE

    def out_spec(width):
        return pl.BlockSpec((1, to, width), lambda b, s: (b, s, 0))

    return pl.pallas_call(
        functools.partial(_out_kernel, alpha=alpha),
        grid=(B, S // to),
        in_specs=[out_spec(D), out_spec(MLA_WIDTH), out_spec(POOL_WIDTH),
                  _const_spec(w_out.shape), _const_spec((1, D)), _const_spec((1, D))],
        out_specs=out_spec(D),
        out_shape=jax.ShapeDtypeStruct((B, S, D), x.dtype),
        compiler_params=pltpu.CompilerParams(
            dimension_semantics=("arbitrary", "arbitrary"),
            vmem_limit_bytes=VMEM_LIMIT_BYTES),
        name="out_proj_layernorm",
    )(x, ya, yb, w_out, ln_g.reshape(1, -1), ln_b.reshape(1, -1))


def kernel(x, positions, w_in, q_norm_g, w_uq, kv_norm_g, w_ukv, pool_w, pool_scale, w_out,
           ln_g, ln_b):
    w1, wqt, wkn, wvt, wp = _prepare_weights(w_in, w_uq, w_ukv, pool_w)
    w_out_b = w_out.astype(_bf16)
    for layer in range(DEPTH):
        x = _layer(x, positions, w1, q_norm_g, wqt, kv_norm_g, wkn, wvt, wp, pool_scale,
                   w_out_b, ln_g[layer], ln_b[layer])
    return x
```

```python
import functools
import math

import jax
import jax.numpy as jnp
import numpy as np
from jax import lax
from jax.experimental import pallas as pl
from jax.experimental.pallas import tpu as pltpu

D_MODEL = 1024
DEPTH = 1
HEADS = 4
NOPE = 128
ROPE = 64
QK_DIM = NOPE + ROPE
V_DIM = 128
Q_RANK = 512
KV_RANK = 256
MLA_WIDTH = HEADS * V_DIM
POOL_WINDOWS = (2, 4, 8, 16)
POOL_GROUP = 128
POOL_WIDTH = len(POOL_WINDOWS) * POOL_GROUP
ROPE_THETA = 10000.0
RMS_EPS = 1e-6
LN_EPS = 1e-5

LANES = 128
MAX_WINDOW = max(POOL_WINDOWS)
VMEM_LIMIT_BYTES = 56 * 1024 * 1024

_C_Q = 0
_C_KV = _C_Q + Q_RANK
_C_KR = _C_KV + KV_RANK
_C_GA = _C_KR + 2 * ROPE
_C_U = _C_GA + MLA_WIDTH
_C_GB = _C_U + POOL_WIDTH
_C_END = _C_GB + POOL_WIDTH

_SCORE_SCALE = (QK_DIM ** -0.5) * math.log2(math.e)
_MASKED = -1e30

TOKEN_TILE = 1024
PROJ_SUBTILE = 512
OUT_TILE = 2048
OUT_SUBTILE = 256
Q_TILE = 512
K_TILE = 512
S_CHUNK = 16
L_ROWS = 16

_bf16 = jnp.bfloat16
_f32 = jnp.float32


def _dot(a, b):
    return jnp.dot(a, b, preferred_element_type=_f32)


def _dot_nt(a, b):
    return lax.dot_general(a, b, (((1,), (1,)), ((), ())), preferred_element_type=_f32)


def _rms_norm(h, g):
    ms = jnp.mean(h * h, axis=-1, keepdims=True)
    return (h * lax.rsqrt(ms + RMS_EPS)) * g


def _silu(z):
    return z * (1.0 / (1.0 + jnp.exp(-z)))


def _proj_kernel(x_ref, pos_ref, freq_ref, w1_ref, qg_ref, wqt_ref, kvg_ref, wkn_ref, wvt_ref,
                 wp_ref, ps_ref,
                 qnt_ref, qrt_ref, kn_ref, kd_ref, vt_ref, ga_ref, yb_ref,
                 carry_ref):
    tm = x_ref.shape[1]
    ts = PROJ_SUBTILE
    si = pl.program_id(1)

    @pl.when(si == 0)
    def _():
        carry_ref[...] = jnp.zeros_like(carry_ref)

    def front(r):
        rows = slice(r, r + ts)
        xb = x_ref[0, rows, :].astype(_bf16)

        ang_t = freq_ref[...] * pos_ref[0, :, rows].astype(_f32)
        cos_t, sin_t = jnp.cos(ang_t), jnp.sin(ang_t)
        tab_t = jnp.concatenate([cos_t, cos_t, sin_t, sin_t], axis=0)
        tab = tab_t.T
        qcos_t, qsin_t = cos_t * _SCORE_SCALE, sin_t * _SCORE_SCALE

        h_q = _dot(xb, w1_ref[:, _C_Q:_C_KV])
        u = _dot(xb, w1_ref[:, _C_U:_C_GB])
        h_kv = _dot(xb, w1_ref[:, _C_KV:_C_KR])
        h_kr = _dot(xb, w1_ref[:, _C_KR:_C_GA])
        h_ga = _dot(xb, w1_ref[:, _C_GA:_C_U])
        h_gb = _dot(xb, w1_ref[:, _C_GB:_C_END])
        return h_q, u, h_kv, h_kr, h_ga, h_gb, tab, qcos_t, qsin_t

    def back(r, products, history):
        rows = slice(r, r + ts)
        h_q, u, h_kv, h_kr, h_ga, h_gb, tab, qcos_t, qsin_t = products

        xqn = _rms_norm(h_q, qg_ref[...])
        qt = _dot_nt(wqt_ref[...], xqn.astype(_bf16))
        qnt_ref[0, :, rows] = (qt[:HEADS * NOPE] * _SCORE_SCALE).astype(_bf16)
        half = ROPE // 2
        for h in range(HEADS):
            lo = HEADS * NOPE + h * ROPE
            t1, t2 = qt[lo:lo + half], qt[lo + half:lo + ROPE]
            roped = jnp.concatenate([t1 * qcos_t - t2 * qsin_t,
                                     t1 * qsin_t + t2 * qcos_t], axis=0)
            qrt_ref[0, h * LANES:h * LANES + ROPE, rows] = roped.astype(_bf16)
            qrt_ref[0, h * LANES + ROPE:(h + 1) * LANES, rows] = jnp.zeros((ROPE, ts), _bf16)

        xkvn = _rms_norm(h_kv, kvg_ref[...]).astype(_bf16)
        kn_ref[0, rows, :] = _dot(xkvn, wkn_ref[...]).astype(_bf16)
        vt_ref[0, :, rows] = _dot_nt(wvt_ref[...], xkvn).astype(_bf16)
        t = h_kr * tab
        kd_ref[0, rows, :] = (t + pltpu.roll(t, ROPE, axis=1)).astype(_bf16)

        ga_ref[0, rows, :] = _silu(h_ga).astype(_bf16)

        ext = jnp.concatenate([history, u], axis=0)
        sums = []
        acc = ext
        width = 1
        for g, w in enumerate(POOL_WINDOWS):
            while width < w:
                acc = acc + pltpu.roll(acc, width, axis=0)
                width *= 2
            sums.append(acc[MAX_WINDOW:, :POOL_GROUP])
            acc = acc[:, POOL_GROUP:]
        tok = si * tm + r + lax.broadcasted_iota(jnp.int32, (ts, 1), 0)
        pooled = []
        for g, w in enumerate(POOL_WINDOWS):
            inv_cnt = 1.0 / jnp.minimum(tok + 1, w).astype(_f32)
            pooled.append(sums[g] * inv_cnt - u[:, g * POOL_GROUP:(g + 1) * POOL_GROUP])
        mixed = jnp.concatenate(
            [_dot(jnp.concatenate(pooled[0:2], axis=1).astype(_bf16), wp_ref[0]),
             _dot(jnp.concatenate(pooled[2:4], axis=1).astype(_bf16), wp_ref[1])], axis=1)
        yb_ref[0, rows, :] = (mixed * ps_ref[...] * _silu(h_gb)).astype(_bf16)
        return u[ts - MAX_WINDOW:, :]

    starts = list(range(0, tm, ts))
    history = carry_ref[...]
    products = front(starts[0])
    for i, r in enumerate(starts):
        upcoming = front(starts[i + 1]) if i + 1 < len(starts) else None
        history = back(r, products, history)
        products = upcoming
    carry_ref[...] = history


def _attn_kernel(qstart_ref, kstart_ref, nfull_ref, nmask_ref,
                 qnt_ref, qrt_ref, kn_ref, kd_ref, vt_ref, ga_ref, pos_ref,
                 o_ref,
                 m_ref, alpha_ref, acc_ref, bias_ref, *head_refs, npairs):
    tq, tk = Q_TILE, K_TILE
    s_refs, smax_refs, p_refs = (head_refs[i * HEADS:(i + 1) * HEADS] for i in range(3))
    nq = o_ref.shape[1] // tq
    b = pl.program_id(0)
    n_full = nfull_ref[b]
    n_all = n_full + nmask_ref[b]

    m_ref[...] = jnp.full(m_ref.shape, _MASKED, _f32)
    acc_ref[...] = jnp.zeros_like(acc_ref)

    def pair(i):
        qs = pl.multiple_of(qstart_ref[b * npairs + i], tq)
        ks = pl.multiple_of(kstart_ref[b * npairs + i], tk)
        return qs, ks

    chunks = [slice(c, c + S_CHUNK) for c in range(0, tk, S_CHUNK)]
    ones_rows = jnp.ones((L_ROWS, tk), _bf16)

    def make_bias(qs, ks):
        for c in range(0, tk, LANES):
            pk_row = pos_ref[0, :, pl.ds(ks + c, LANES)]
            pk_col = jnp.broadcast_to(pk_row, (LANES, LANES)).T
            for j in range(0, tq, LANES):
                keep = pk_col <= pos_ref[0, :, pl.ds(qs + j, LANES)]
                bias_ref[c:c + LANES, j:j + LANES] = jnp.where(keep, 0.0, _MASKED)

    def scores(h, qs, ks, masked):
        rows = slice(h * LANES, (h + 1) * LANES)
        qc = jnp.concatenate([qnt_ref[0, rows, pl.ds(qs, tq)],
                              qrt_ref[0, rows, pl.ds(qs, tq)]], axis=0)
        kc = jnp.concatenate([kn_ref[0, pl.ds(ks, tk), rows],
                              kd_ref[0, pl.ds(ks, tk), :]], axis=1)
        s = _dot(kc, qc)
        if masked:
            s = s + bias_ref[...]
        s_refs[h][...] = s
        mx = s[chunks[0], :]
        for c in chunks[1:]:
            mx = jnp.maximum(mx, s[c, :])
        smax_refs[h][...] = mx

    def softmax(h, qs):
        cols = pl.ds(qs, tq)
        sb = s_refs[h]
        m_old = m_ref[h, :, cols]
        m_new = jnp.maximum(m_old, jnp.max(smax_refs[h][...], axis=0, keepdims=True))
        alpha_ref[h] = jnp.exp2(m_old - m_new)
        m_ref[h, :, cols] = m_new
        for c in chunks:
            p_refs[h][c, :] = jnp.exp2(sb[c, :] - m_new).astype(_bf16)

    def pv(h, qs, ks):
        rows = slice(h * LANES, (h + 1) * LANES)
        cols = pl.ds(qs, tq)
        v_ext = jnp.concatenate([vt_ref[0, rows, pl.ds(ks, tk)], ones_rows], axis=0)
        acc_ref[h, :, cols] = alpha_ref[h] * acc_ref[h, :, cols] + _dot(v_ext, p_refs[h][...])

    def step(i, next_masked):
        qs, ks = pair(i)
        if next_masked is not None:
            qs_n, ks_n = pair(i + 1)
            if next_masked:
                make_bias(qs_n, ks_n)
        for h in range(HEADS):
            if next_masked is not None:
                scores(h, qs_n, ks_n, next_masked)
            pv(h, qs, ks)
            if h + 1 < HEADS:
                softmax(h + 1, qs)
            elif next_masked is not None:
                softmax(0, qs_n)

    qs0, ks0 = pair(0)

    @pl.when(n_full > 0)
    def _():
        for h in range(HEADS):
            scores(h, qs0, ks0, False)

    @pl.when(n_full == 0)
    def _():
        make_bias(qs0, ks0)
        for h in range(HEADS):
            scores(h, qs0, ks0, True)

    softmax(0, qs0)

    def step_next_full(i, carry):
        step(i, False)
        return carry

    def step_next_masked(i, carry):
        step(i, True)
        return carry

    first_masked_next = jnp.maximum(n_full - 1, 0)
    lax.fori_loop(0, first_masked_next, step_next_full, 0)
    lax.fori_loop(first_masked_next, n_all - 1, step_next_masked, 0)
    step(n_all - 1, None)

    for qt in range(nq):
        rows = slice(qt * tq, (qt + 1) * tq)
        for h in range(HEADS):
            cols = slice(h * LANES, (h + 1) * LANES)
            inv_l = 1.0 / acc_ref[h, V_DIM:V_DIM + 1, rows]
            o = (acc_ref[h, :V_DIM, rows] * inv_l).T
            o_ref[0, rows, cols] = (o * ga_ref[0, rows, cols].astype(_f32)).astype(_bf16)


def _out_kernel(x_ref, ya_ref, yb_ref, wo_ref, g_ref, b_ref, o_ref, *, alpha):
    for r in range(0, x_ref.shape[1], OUT_SUBTILE):
        rows = slice(r, r + OUT_SUBTILE)
        y = jnp.concatenate([ya_ref[0, rows, :], yb_ref[0, rows, :]], axis=1)
        z = alpha * x_ref[0, rows, :] + _dot(y, wo_ref[...])
        mu = jnp.mean(z, axis=-1, keepdims=True)
        zc = z - mu
        var = jnp.mean(zc * zc, axis=-1, keepdims=True)
        o_ref[0, rows, :] = zc * lax.rsqrt(var + LN_EPS) * g_ref[...] + b_ref[...]


def _rotate_half_cols(w):
    half = w.shape[-1] // 2
    return jnp.concatenate([-w[..., half:], w[..., :half]], axis=-1)


def _prepare_weights(w_in, w_uq, w_ukv, pool_w):
    w_in, w_uq, w_ukv, pool_w = (w.astype(_bf16) for w in (w_in, w_uq, w_ukv, pool_w))
    splits = np.cumsum([Q_RANK, KV_RANK, ROPE, MLA_WIDTH, POOL_WIDTH, POOL_WIDTH])[:-1]
    wq_l, wkv_l, wkr, wga, wu, wgb = jnp.split(w_in, [int(c) for c in splits], axis=1)
    w1 = jnp.concatenate([wq_l, wkv_l, wkr, _rotate_half_cols(wkr), wga, wu, wgb], axis=1)

    wq3 = w_uq.reshape(Q_RANK, HEADS, QK_DIM)
    wq_nope = wq3[:, :, :NOPE].reshape(Q_RANK, HEADS * NOPE)
    wq_rope = wq3[:, :, NOPE:].reshape(Q_RANK, HEADS * ROPE)
    wqt = jnp.concatenate([wq_nope, wq_rope], axis=1).T

    wkv3 = w_ukv.reshape(KV_RANK, HEADS, NOPE + V_DIM)
    wkn = wkv3[:, :, :NOPE].reshape(KV_RANK, HEADS * NOPE)
    wvt = wkv3[:, :, NOPE:].reshape(KV_RANK, HEADS * V_DIM).T

    z = jnp.zeros((POOL_GROUP, POOL_GROUP), pool_w.dtype)
    wp = jnp.stack([jnp.block([[pool_w[0], z], [z, pool_w[1]]]),
                    jnp.block([[pool_w[2], z], [z, pool_w[3]]])])
    return w1, wqt, wkn, wvt, wp


def _const_spec(shape):
    return pl.BlockSpec(shape, lambda *_: (0,) * len(shape), pipeline_mode=pl.Buffered(1))


def _layer(x, positions, w1, q_norm_g, wqt, kv_norm_g, wkn, wvt, wp, pool_scale, w_out,
           ln_g, ln_b):
    B, S, D = x.shape
    tm, tq, tk = TOKEN_TILE, Q_TILE, K_TILE
    ns, nq, nk = S // tm, S // tq, S // tk

    half = ROPE // 2
    inv_freq = ROPE_THETA ** (-jnp.arange(half, dtype=_f32) / half)
    freq_col = inv_freq.reshape(half, 1)
    pos_row = positions.reshape(B, 1, S)

    def tok_spec(width):
        return pl.BlockSpec((1, tm, width), lambda b, s: (b, s, 0))

    def feat_spec(width):
        return pl.BlockSpec((1, width, tm), lambda b, s: (b, 0, s))

    act = lambda width: jax.ShapeDtypeStruct((B, S, width), _bf16)
    act_t = lambda width: jax.ShapeDtypeStruct((B, width, S), _bf16)
    qnt, qrt, kn, kd, vt, ga, yb = pl.pallas_call(
        _proj_kernel,
        grid=(B, ns),
        in_specs=[tok_spec(D), feat_spec(1), _const_spec((half, 1)),
                  _const_spec(w1.shape), _const_spec((1, Q_RANK)), _const_spec(wqt.shape),
                  _const_spec((1, KV_RANK)), _const_spec(wkn.shape), _const_spec(wvt.shape),
                  _const_spec(wp.shape), _const_spec((1, POOL_WIDTH))],
        out_specs=[feat_spec(MLA_WIDTH), feat_spec(MLA_WIDTH), tok_spec(MLA_WIDTH),
                   tok_spec(LANES), feat_spec(MLA_WIDTH), tok_spec(MLA_WIDTH),
                   tok_spec(POOL_WIDTH)],
        out_shape=[act_t(MLA_WIDTH), act_t(MLA_WIDTH), act(MLA_WIDTH), act(LANES),
                   act_t(MLA_WIDTH), act(MLA_WIDTH), act(POOL_WIDTH)],
        scratch_shapes=[pltpu.VMEM((MAX_WINDOW, POOL_WIDTH), _f32)],
        compiler_params=pltpu.CompilerParams(
            dimension_semantics=("arbitrary", "arbitrary"),
            vmem_limit_bytes=VMEM_LIMIT_BYTES),
        name="mla_pool_proj",
    )(x, pos_row, freq_col, w1, q_norm_g.reshape(1, -1), wqt, kv_norm_g.reshape(1, -1), wkn,
      wvt, wp, pool_scale.reshape(1, -1))

    pq_blk = positions.reshape(B, nq, tq)
    pk_blk = positions.reshape(B, nk, tk)
    need = pq_blk.max(-1)[:, :, None] >= pk_blk.min(-1)[:, None, :]
    full = pq_blk.min(-1)[:, :, None] >= pk_blk.max(-1)[:, None, :]
    npairs = nq * nk
    rank = jnp.where(need & full, 0, jnp.where(need, 1, 2)).reshape(B, npairs)
    order = jnp.argsort(rank, axis=1, stable=True).astype(jnp.int32)
    q_start = (order // nk) * tq
    k_start = (order % nk) * tk
    n_full = jnp.sum(rank == 0, axis=1, dtype=jnp.int32)
    n_mask = jnp.sum(rank == 1, axis=1, dtype=jnp.int32)

    row_spec = lambda d1, d2: pl.BlockSpec((1, d1, d2), lambda b, *_: (b, 0, 0))
    ya = pl.pallas_call(
        functools.partial(_attn_kernel, npairs=npairs),
        grid_spec=pltpu.PrefetchScalarGridSpec(
            num_scalar_prefetch=4,
            grid=(B,),
            in_specs=[row_spec(MLA_WIDTH, S), row_spec(MLA_WIDTH, S), row_spec(S, MLA_WIDTH),
                      row_spec(S, LANES), row_spec(MLA_WIDTH, S), row_spec(S, MLA_WIDTH),
                      row_spec(1, S)],
            out_specs=row_spec(S, MLA_WIDTH),
            scratch_shapes=[pltpu.VMEM((HEADS, 1, S), _f32),
                            pltpu.VMEM((HEADS, 1, tq), _f32),
                            pltpu.VMEM((HEADS, V_DIM + L_ROWS, S), _f32),
                            pltpu.VMEM((tk, tq), _f32)]
            + [pltpu.VMEM((tk, tq), _f32) for _ in range(HEADS)]
            + [pltpu.VMEM((S_CHUNK, tq), _f32) for _ in range(HEADS)]
            + [pltpu.VMEM((tk, tq), _bf16) for _ in range(HEADS)]),
        out_shape=act(MLA_WIDTH),
        compiler_params=pltpu.CompilerParams(
            dimension_semantics=("arbitrary",),
            vmem_limit_bytes=VMEM_LIMIT_BYTES),
        name="mla_attention",
    )(q_start.reshape(-1), k_start.reshape(-1), n_full, n_mask,
      qnt, qrt, kn, kd, vt, ga, pos_row)

    alpha = (2.0 * DEPTH) ** 0.25
    to = OUT_TILE

    def out_spec(width):
        return pl.BlockSpec((1, to, width), lambda b, s: (b, s, 0))

    return pl.pallas_call(
        functools.partial(_out_kernel, alpha=alpha),
        grid=(B, S // to),
        in_specs=[out_spec(D), out_spec(MLA_WIDTH), out_spec(POOL_WIDTH),
                  _const_spec(w_out.shape), _const_spec((1, D)), _const_spec((1, D))],
        out_specs=out_spec(D),
        out_shape=jax.ShapeDtypeStruct((B, S, D), x.dtype),
        compiler_params=pltpu.CompilerParams(
            dimension_semantics=("arbitrary", "arbitrary"),
            vmem_limit_bytes=VMEM_LIMIT_BYTES),
        name="out_proj_layernorm",
    )(x, ya, yb, w_out, ln_g.reshape(1, -1), ln_b.reshape(1, -1))


def kernel(x, positions, w_in, q_norm_g, w_uq, kv_norm_g, w_ukv, pool_w, pool_scale, w_out,
           ln_g, ln_b):
    w1, wqt, wkn, wvt, wp = _prepare_weights(w_in, w_uq, w_ukv, pool_w)
    w_out_b = w_out.astype(_bf16)
    for layer in range(DEPTH):
        x = _layer(x, positions, w1, q_norm_g, wqt, kv_norm_g, wkn, wvt, wp, pool_scale,
                   w_out_b, ln_g[layer], ln_b[layer])
    return x
```

```python
import functools
import math

import jax
import jax.numpy as jnp
import numpy as np
from jax import lax
from jax.experimental import pallas as pl
from jax.experimental.pallas import tpu as pltpu

D_MODEL = 1024
DEPTH = 1
HEADS = 4
NOPE = 128
ROPE = 64
QK_DIM = NOPE + ROPE
V_DIM = 128
Q_RANK = 512
KV_RANK = 256
MLA_WIDTH = HEADS * V_DIM
POOL_WINDOWS = (2, 4, 8, 16)
POOL_GROUP = 128
POOL_WIDTH = len(POOL_WINDOWS) * POOL_GROUP
ROPE_THETA = 10000.0
RMS_EPS = 1e-6
LN_EPS = 1e-5

LANES = 128
MAX_WINDOW = max(POOL_WINDOWS)
VMEM_LIMIT_BYTES = 56 * 1024 * 1024

_C_Q = 0
_C_KV = _C_Q + Q_RANK
_C_KR = _C_KV + KV_RANK
_C_GA = _C_KR + 2 * ROPE
_C_U = _C_GA + MLA_WIDTH
_C_GB = _C_U + POOL_WIDTH
_C_END = _C_GB + POOL_WIDTH

_SCORE_SCALE = (QK_DIM ** -0.5) * math.log2(math.e)
_MASKED = -1e30

TOKEN_TILE = 1024
PROJ_SUBTILE = 512
OUT_SUBTILE = 256
Q_TILE = 512
K_TILE = 512
S_CHUNK = 16
L_ROWS = 16

_bf16 = jnp.bfloat16
_f32 = jnp.float32


def _dot(a, b):
    return jnp.dot(a, b, preferred_element_type=_f32)


def _dot_nt(a, b):
    return lax.dot_general(a, b, (((1,), (1,)), ((), ())), preferred_element_type=_f32)


def _rms_norm(h, g):
    ms = jnp.mean(h * h, axis=-1, keepdims=True)
    return (h * lax.rsqrt(ms + RMS_EPS)) * g


def _silu(z):
    return z * (1.0 / (1.0 + jnp.exp(-z)))


def _proj_kernel(x_ref, pos_ref, freq_ref, w1_ref, qg_ref, wqt_ref, kvg_ref, wkn_ref, wvt_ref,
                 wp_ref, ps_ref,
                 qnt_ref, qrt_ref, kn_ref, kd_ref, vt_ref, ga_ref, yb_ref,
                 carry_ref):
    tm = x_ref.shape[1]
    ts = PROJ_SUBTILE
    si = pl.program_id(1)

    @pl.when(si == 0)
    def _():
        carry_ref[...] = jnp.zeros_like(carry_ref)

    def front(r):
        rows = slice(r, r + ts)
        xb = x_ref[0, rows, :].astype(_bf16)

        ang_t = freq_ref[...] * pos_ref[0, :, rows].astype(_f32)
        cos_t, sin_t = jnp.cos(ang_t), jnp.sin(ang_t)
        tab_t = jnp.concatenate([cos_t, cos_t, sin_t, sin_t], axis=0)
        tab = tab_t.T
        qcos_t, qsin_t = cos_t * _SCORE_SCALE, sin_t * _SCORE_SCALE

        h_q = _dot(xb, w1_ref[:, _C_Q:_C_KV])
        u = _dot(xb, w1_ref[:, _C_U:_C_GB])
        h_kv = _dot(xb, w1_ref[:, _C_KV:_C_KR])
        h_kr = _dot(xb, w1_ref[:, _C_KR:_C_GA])
        h_ga = _dot(xb, w1_ref[:, _C_GA:_C_U])
        h_gb = _dot(xb, w1_ref[:, _C_GB:_C_END])
        return h_q, u, h_kv, h_kr, h_ga, h_gb, tab, qcos_t, qsin_t

    def back(r, products, history):
        rows = slice(r, r + ts)
        h_q, u, h_kv, h_kr, h_ga, h_gb, tab, qcos_t, qsin_t = products

        xqn = _rms_norm(h_q, qg_ref[...])
        qt = _dot_nt(wqt_ref[...], xqn.astype(_bf16))
        qnt_ref[0, :, rows] = (qt[:HEADS * NOPE] * _SCORE_SCALE).astype(_bf16)
        half = ROPE // 2
        for h in range(HEADS):
            lo = HEADS * NOPE + h * ROPE
            t1, t2 = qt[lo:lo + half], qt[lo + half:lo + ROPE]
            roped = jnp.concatenate([t1 * qcos_t - t2 * qsin_t,
                                     t1 * qsin_t + t2 * qcos_t], axis=0)
            qrt_ref[0, h * LANES:h * LANES + ROPE, rows] = roped.astype(_bf16)
            qrt_ref[0, h * LANES + ROPE:(h + 1) * LANES, rows] = jnp.zeros((ROPE, ts), _bf16)

        xkvn = _rms_norm(h_kv, kvg_ref[...]).astype(_bf16)
        kn_ref[0, rows, :] = _dot(xkvn, wkn_ref[...]).astype(_bf16)
        vt_ref[0, :, rows] = _dot_nt(wvt_ref[...], xkvn).astype(_bf16)
        t = h_kr * tab
        kd_ref[0, rows, :] = (t + pltpu.roll(t, ROPE, axis=1)).astype(_bf16)

        ga_ref[0, rows, :] = _silu(h_ga).astype(_bf16)

        ext = jnp.concatenate([history, u], axis=0)
        sums = []
        acc = ext
        width = 1
        for g, w in enumerate(POOL_WINDOWS):
            while width < w:
                acc = acc + pltpu.roll(acc, width, axis=0)
                width *= 2
            sums.append(acc[MAX_WINDOW:, :POOL_GROUP])
            acc = acc[:, POOL_GROUP:]
        tok = si * tm + r + lax.broadcasted_iota(jnp.int32, (ts, 1), 0)
        pooled = []
        for g, w in enumerate(POOL_WINDOWS):
            inv_cnt = 1.0 / jnp.minimum(tok + 1, w).astype(_f32)
            pooled.append(sums[g] * inv_cnt - u[:, g * POOL_GROUP:(g + 1) * POOL_GROUP])
        mixed = jnp.concatenate(
            [_dot(jnp.concatenate(pooled[0:2], axis=1).astype(_bf16), wp_ref[0]),
             _dot(jnp.concatenate(pooled[2:4], axis=1).astype(_bf16), wp_ref[1])], axis=1)
        yb_ref[0, rows, :] = (mixed * ps_ref[...] * _silu(h_gb)).astype(_bf16)
        return u[ts - MAX_WINDOW:, :]

    starts = list(range(0, tm, ts))
    history = carry_ref[...]
    products = front(starts[0])
    for i, r in enumerate(starts):
        upcoming = front(starts[i + 1]) if i + 1 < len(starts) else None
        history = back(r, products, history)
        products = upcoming
    carry_ref[...] = history


def _attn_out_kernel(qstart_ref, kstart_ref, nfull_ref, nmask_ref,
                     qnt_ref, qrt_ref, kn_ref, kd_ref, vt_ref, ga_ref, pos_ref,
                     yb_ref, wo_ref, lng_ref, lnb_ref, x_hbm,
                     out_hbm,
                     m_ref, alpha_ref, acc_ref, bias_ref, xbuf_ref, obuf_ref, in_sem, out_sem,
                     *head_refs, npairs, nrows, res_scale):
    tq, tk = Q_TILE, K_TILE
    s_refs, smax_refs, p_refs = (head_refs[i * HEADS:(i + 1) * HEADS] for i in range(3))
    nq = ga_ref.shape[1] // tq
    b = pl.program_id(0)
    n_full = nfull_ref[b]
    n_all = n_full + nmask_ref[b]

    def x_copy(qt, slot):
        return pltpu.make_async_copy(x_hbm.at[b, pl.ds(qt * tq, tq), :], xbuf_ref.at[slot],
                                     in_sem.at[slot])

    def out_copy(qt, slot):
        return pltpu.make_async_copy(obuf_ref.at[slot], out_hbm.at[b, pl.ds(qt * tq, tq), :],
                                     out_sem.at[slot])

    for qt in range(min(2, nq)):
        x_copy(qt, qt).start()

    m_ref[...] = jnp.full(m_ref.shape, _MASKED, _f32)
    acc_ref[...] = jnp.zeros_like(acc_ref)

    def pair(i):
        qs = pl.multiple_of(qstart_ref[b * npairs + i], tq)
        ks = pl.multiple_of(kstart_ref[b * npairs + i], tk)
        return qs, ks

    chunks = [slice(c, c + S_CHUNK) for c in range(0, tk, S_CHUNK)]
    ones_rows = jnp.ones((L_ROWS, tk), _bf16)

    def make_bias(qs, ks):
        for c in range(0, tk, LANES):
            pk_row = pos_ref[0, :, pl.ds(ks + c, LANES)]
            pk_col = jnp.broadcast_to(pk_row, (LANES, LANES)).T
            for j in range(0, tq, LANES):
                keep = pk_col <= pos_ref[0, :, pl.ds(qs + j, LANES)]
                bias_ref[c:c + LANES, j:j + LANES] = jnp.where(keep, 0.0, _MASKED)

    def scores(h, qs, ks, masked):
        rows = slice(h * LANES, (h + 1) * LANES)
        qc = jnp.concatenate([qnt_ref[0, rows, pl.ds(qs, tq)],
                              qrt_ref[0, rows, pl.ds(qs, tq)]], axis=0)
        kc = jnp.concatenate([kn_ref[0, pl.ds(ks, tk), rows],
                              kd_ref[0, pl.ds(ks, tk), :]], axis=1)
        s = _dot(kc, qc)
        if masked:
            s = s + bias_ref[...]
        s_refs[h][...] = s
        mx = s[chunks[0], :]
        for c in chunks[1:]:
            mx = jnp.maximum(mx, s[c, :])
        smax_refs[h][...] = mx

    def softmax(h, qs):
        cols = pl.ds(qs, tq)
        sb = s_refs[h]
        m_old = m_ref[h, :, cols]
        m_new = jnp.maximum(m_old, jnp.max(smax_refs[h][...], axis=0, keepdims=True))
        alpha_ref[h] = jnp.exp2(m_old - m_new)
        m_ref[h, :, cols] = m_new
        for c in chunks:
            p_refs[h][c, :] = jnp.exp2(sb[c, :] - m_new).astype(_bf16)

    def pv(h, qs, ks):
        rows = slice(h * LANES, (h + 1) * LANES)
        cols = pl.ds(qs, tq)
        v_ext = jnp.concatenate([vt_ref[0, rows, pl.ds(ks, tk)], ones_rows], axis=0)
        acc_ref[h, :, cols] = alpha_ref[h] * acc_ref[h, :, cols] + _dot(v_ext, p_refs[h][...])

    def step(i, next_masked):
        qs, ks = pair(i)
        if next_masked is not None:
            qs_n, ks_n = pair(i + 1)
            if next_masked:
                make_bias(qs_n, ks_n)
        for h in range(HEADS):
            if next_masked is not None:
                scores(h, qs_n, ks_n, next_masked)
            pv(h, qs, ks)
            if h + 1 < HEADS:
                softmax(h + 1, qs)
            elif next_masked is not None:
                softmax(0, qs_n)

    qs0, ks0 = pair(0)

    @pl.when(n_full > 0)
    def _():
        for h in range(HEADS):
            scores(h, qs0, ks0, False)

    @pl.when(n_full == 0)
    def _():
        make_bias(qs0, ks0)
        for h in range(HEADS):
            scores(h, qs0, ks0, True)

    softmax(0, qs0)

    def step_next_full(i, carry):
        step(i, False)
        return carry

    def step_next_masked(i, carry):
        step(i, True)
        return carry

    first_masked_next = jnp.maximum(n_full - 1, 0)
    lax.fori_loop(0, first_masked_next, step_next_full, 0)
    lax.fori_loop(first_masked_next, n_all - 1, step_next_masked, 0)
    step(n_all - 1, None)

    for qt in range(nq):
        slot = qt % 2
        rows = slice(qt * tq, (qt + 1) * tq)
        ya = []
        for h in range(HEADS):
            cols = slice(h * LANES, (h + 1) * LANES)
            inv_l = 1.0 / acc_ref[h, V_DIM:V_DIM + 1, rows]
            o = (acc_ref[h, :V_DIM, rows] * inv_l).T
            ya.append((o * ga_ref[0, rows, cols].astype(_f32)).astype(_bf16))
        y = jnp.concatenate(ya + [yb_ref[0, rows, :]], axis=1)

        x_copy(qt, slot).wait()
        if qt >= 2:
            out_copy(qt - 2, slot).wait()
        else:
            @pl.when(b > 0)
            def _():
                out_copy(qt, slot).wait()
        for r in range(0, tq, OUT_SUBTILE):
            sub = slice(r, r + OUT_SUBTILE)
            z = res_scale * xbuf_ref[slot, sub, :] + _dot(y[sub], wo_ref[...])
            mu = jnp.mean(z, axis=-1, keepdims=True)
            zc = z - mu
            var = jnp.mean(zc * zc, axis=-1, keepdims=True)
            obuf_ref[slot, sub, :] = zc * lax.rsqrt(var + LN_EPS) * lng_ref[...] + lnb_ref[...]
        out_copy(qt, slot).start()
        if qt + 2 < nq:
            x_copy(qt + 2, slot).start()

    @pl.when(b == nrows - 1)
    def _():
        for qt in range(max(nq - 2, 0), nq):
            out_copy(qt, qt % 2).wait()


def _rotate_half_cols(w):
    half = w.shape[-1] // 2
    return jnp.concatenate([-w[..., half:], w[..., :half]], axis=-1)


def _prepare_weights(w_in, w_uq, w_ukv, pool_w):
    w_in, w_uq, w_ukv, pool_w = (w.astype(_bf16) for w in (w_in, w_uq, w_ukv, pool_w))
    splits = np.cumsum([Q_RANK, KV_RANK, ROPE, MLA_WIDTH, POOL_WIDTH, POOL_WIDTH])[:-1]
    wq_l, wkv_l, wkr, wga, wu, wgb = jnp.split(w_in, [int(c) for c in splits], axis=1)
    w1 = jnp.concatenate([wq_l, wkv_l, wkr, _rotate_half_cols(wkr), wga, wu, wgb], axis=1)

    wq3 = w_uq.reshape(Q_RANK, HEADS, QK_DIM)
    wq_nope = wq3[:, :, :NOPE].reshape(Q_RANK, HEADS * NOPE)
    wq_rope = wq3[:, :, NOPE:].reshape(Q_RANK, HEADS * ROPE)
    wqt = jnp.concatenate([wq_nope, wq_rope], axis=1).T

    wkv3 = w_ukv.reshape(KV_RANK, HEADS, NOPE + V_DIM)
    wkn = wkv3[:, :, :NOPE].reshape(KV_RANK, HEADS * NOPE)
    wvt = wkv3[:, :, NOPE:].reshape(KV_RANK, HEADS * V_DIM).T

    z = jnp.zeros((POOL_GROUP, POOL_GROUP), pool_w.dtype)
    wp = jnp.stack([jnp.block([[pool_w[0], z], [z, pool_w[1]]]),
                    jnp.block([[pool_w[2], z], [z, pool_w[3]]])])
    return w1, wqt, wkn, wvt, wp


def _const_spec(shape):
    return pl.BlockSpec(shape, lambda *_: (0,) * len(shape), pipeline_mode=pl.Buffered(1))


def _layer(x, positions, w1, q_norm_g, wqt, kv_norm_g, wkn, wvt, wp, pool_scale, w_out,
           ln_g, ln_b):
    B, S, D = x.shape
    tm, tq, tk = TOKEN_TILE, Q_TILE, K_TILE
    ns, nq, nk = S // tm, S // tq, S // tk

    half = ROPE // 2
    inv_freq = ROPE_THETA ** (-jnp.arange(half, dtype=_f32) / half)
    freq_col = inv_freq.reshape(half, 1)
    pos_row = positions.reshape(B, 1, S)

    def tok_spec(width):
        return pl.BlockSpec((1, tm, width), lambda b, s: (b, s, 0))

    def feat_spec(width):
        return pl.BlockSpec((1, width, tm), lambda b, s: (b, 0, s))

    act = lambda width: jax.ShapeDtypeStruct((B, S, width), _bf16)
    act_t = lambda width: jax.ShapeDtypeStruct((B, width, S), _bf16)
    qnt, qrt, kn, kd, vt, ga, yb = pl.pallas_call(
        _proj_kernel,
        grid=(B, ns),
        in_specs=[tok_spec(D), feat_spec(1), _const_spec((half, 1)),
                  _const_spec(w1.shape), _const_spec((1, Q_RANK)), _const_spec(wqt.shape),
                  _const_spec((1, KV_RANK)), _const_spec(wkn.shape), _const_spec(wvt.shape),
                  _const_spec(wp.shape), _const_spec((1, POOL_WIDTH))],
        out_specs=[feat_spec(MLA_WIDTH), feat_spec(MLA_WIDTH), tok_spec(MLA_WIDTH),
                   tok_spec(LANES), feat_spec(MLA_WIDTH), tok_spec(MLA_WIDTH),
                   tok_spec(POOL_WIDTH)],
        out_shape=[act_t(MLA_WIDTH), act_t(MLA_WIDTH), act(MLA_WIDTH), act(LANES),
                   act_t(MLA_WIDTH), act(MLA_WIDTH), act(POOL_WIDTH)],
        scratch_shapes=[pltpu.VMEM((MAX_WINDOW, POOL_WIDTH), _f32)],
        compiler_params=pltpu.CompilerParams(
            dimension_semantics=("arbitrary", "arbitrary"),
            vmem_limit_bytes=VMEM_LIMIT_BYTES),
        name="mla_pool_proj",
    )(x, pos_row, freq_col, w1, q_norm_g.reshape(1, -1), wqt, kv_norm_g.reshape(1, -1), wkn,
      wvt, wp, pool_scale.reshape(1, -1))

    pq_blk = positions.reshape(B, nq, tq)
    pk_blk = positions.reshape(B, nk, tk)
    need = pq_blk.max(-1)[:, :, None] >= pk_blk.min(-1)[:, None, :]
    full = pq_blk.min(-1)[:, :, None] >= pk_blk.max(-1)[:, None, :]
    npairs = nq * nk
    rank = jnp.where(need & full, 0, jnp.where(need, 1, 2)).reshape(B, npairs)
    order = jnp.argsort(rank, axis=1, stable=True).astype(jnp.int32)
    q_start = (order // nk) * tq
    k_start = (order % nk) * tk
    n_full = jnp.sum(rank == 0, axis=1, dtype=jnp.int32)
    n_mask = jnp.sum(rank == 1, axis=1, dtype=jnp.int32)

    row_spec = lambda d1, d2: pl.BlockSpec((1, d1, d2), lambda b, *_: (b, 0, 0))
    alpha = (2.0 * DEPTH) ** 0.25
    return pl.pallas_call(
        functools.partial(_attn_out_kernel, npairs=npairs, nrows=B, res_scale=alpha),
        grid_spec=pltpu.PrefetchScalarGridSpec(
            num_scalar_prefetch=4,
            grid=(B,),
            in_specs=[row_spec(MLA_WIDTH, S), row_spec(MLA_WIDTH, S), row_spec(S, MLA_WIDTH),
                      row_spec(S, LANES), row_spec(MLA_WIDTH, S), row_spec(S, MLA_WIDTH),
                      row_spec(1, S), row_spec(S, POOL_WIDTH),
                      _const_spec(w_out.shape), _const_spec((1, D)), _const_spec((1, D)),
                      pl.BlockSpec(memory_space=pl.ANY)],
            out_specs=pl.BlockSpec(memory_space=pl.ANY),
            scratch_shapes=[pltpu.VMEM((HEADS, 1, S), _f32),
                            pltpu.VMEM((HEADS, 1, tq), _f32),
                            pltpu.VMEM((HEADS, V_DIM + L_ROWS, S), _f32),
                            pltpu.VMEM((tk, tq), _f32),
                            pltpu.VMEM((2, tq, D), _f32),
                            pltpu.VMEM((2, tq, D), _f32),
                            pltpu.SemaphoreType.DMA((2,)),
                            pltpu.SemaphoreType.DMA((2,))]
            + [pltpu.VMEM((tk, tq), _f32) for _ in range(HEADS)]
            + [pltpu.VMEM((S_CHUNK, tq), _f32) for _ in range(HEADS)]
            + [pltpu.VMEM((tk, tq), _bf16) for _ in range(HEADS)]),
        out_shape=jax.ShapeDtypeStruct((B, S, D), x.dtype),
        compiler_params=pltpu.CompilerParams(
            dimension_semantics=("arbitrary",),
            vmem_limit_bytes=VMEM_LIMIT_BYTES),
        name="mla_attention_out",
    )(q_start.reshape(-1), k_start.reshape(-1), n_full, n_mask,
      qnt, qrt, kn, kd, vt, ga, pos_row, yb, w_out, ln_g.reshape(1, -1), ln_b.reshape(1, -1), x)


def kernel(x, positions, w_in, q_norm_g, w_uq, kv_norm_g, w_ukv, pool_w, pool_scale, w_out,
           ln_g, ln_b):
    w1, wqt, wkn, wvt, wp = _prepare_weights(w_in, w_uq, w_ukv, pool_w)
    w_out_b = w_out.astype(_bf16)
    for layer in range(DEPTH):
        x = _layer(x, positions, w1, q_norm_g, wqt, kv_norm_g, wkn, wvt, wp, pool_scale,
                   w_out_b, ln_g[layer], ln_b[layer])
    return x
```

```python
import functools
import math

import jax
import jax.numpy as jnp
import numpy as np
from jax import lax
from jax.experimental import pallas as pl
from jax.experimental.pallas import tpu as pltpu

D_MODEL = 1024
DEPTH = 1
HEADS = 4
NOPE = 128
ROPE = 64
QK_DIM = NOPE + ROPE
V_DIM = 128
Q_RANK = 512
KV_RANK = 256
MLA_WIDTH = HEADS * V_DIM
POOL_WINDOWS = (2, 4, 8, 16)
POOL_GROUP = 128
POOL_WIDTH = len(POOL_WINDOWS) * POOL_GROUP
ROPE_THETA = 10000.0
RMS_EPS = 1e-6
LN_EPS = 1e-5

LANES = 128
MAX_WINDOW = max(POOL_WINDOWS)
VMEM_LIMIT_BYTES = 60 * 1024 * 1024

_C_Q = 0
_C_KV = _C_Q + Q_RANK
_C_KR = _C_KV + KV_RANK
_C_GA = _C_KR + 2 * ROPE
_C_U = _C_GA + MLA_WIDTH
_C_GB = _C_U + POOL_WIDTH
_C_END = _C_GB + POOL_WIDTH

_SCORE_SCALE = (QK_DIM ** -0.5) * math.log2(math.e)
_MASKED = -1e30

TOKEN_TILE = 1024
PROJ_SUBTILE = 512
OUT_SUBTILE = 256
Q_TILE = 512
K_TILE = 512
S_CHUNK = 16
L_ROWS = 16

_bf16 = jnp.bfloat16
_f32 = jnp.float32


def _dot(a, b):
    return jnp.dot(a, b, preferred_element_type=_f32)


def _dot_nt(a, b):
    return lax.dot_general(a, b, (((1,), (1,)), ((), ())), preferred_element_type=_f32)


def _rms_norm(h, g):
    ms = jnp.mean(h * h, axis=-1, keepdims=True)
    return (h * lax.rsqrt(ms + RMS_EPS)) * g


def _silu(z):
    return z * (1.0 / (1.0 + jnp.exp(-z)))


def _proj_kernel(x_ref, pos_ref, freq_ref, w1_ref, qg_ref, wqt_ref, kvg_ref, wkn_ref, wvt_ref,
                 wp_ref, ps_ref,
                 qnt_ref, qrt_ref, kn_ref, kd_ref, vt_ref, ga_ref, yb_ref,
                 carry_ref):
    tm = x_ref.shape[1]
    ts = PROJ_SUBTILE
    si = pl.program_id(1)

    @pl.when(si == 0)
    def _():
        carry_ref[...] = jnp.zeros_like(carry_ref)

    def front(r):
        rows = slice(r, r + ts)
        xb = x_ref[0, rows, :].astype(_bf16)

        ang_t = freq_ref[...] * pos_ref[0, :, rows].astype(_f32)
        cos_t, sin_t = jnp.cos(ang_t), jnp.sin(ang_t)
        tab_t = jnp.concatenate([cos_t, cos_t, sin_t, sin_t], axis=0)
        tab = tab_t.T
        qcos_t, qsin_t = cos_t * _SCORE_SCALE, sin_t * _SCORE_SCALE

        h_q = _dot(xb, w1_ref[:, _C_Q:_C_KV])
        u = _dot(xb, w1_ref[:, _C_U:_C_GB])
        h_kv = _dot(xb, w1_ref[:, _C_KV:_C_KR])
        h_kr = _dot(xb, w1_ref[:, _C_KR:_C_GA])
        h_ga = _dot(xb, w1_ref[:, _C_GA:_C_U])
        h_gb = _dot(xb, w1_ref[:, _C_GB:_C_END])
        return h_q, u, h_kv, h_kr, h_ga, h_gb, tab, qcos_t, qsin_t

    def back(r, products, history):
        rows = slice(r, r + ts)
        h_q, u, h_kv, h_kr, h_ga, h_gb, tab, qcos_t, qsin_t = products

        xqn = _rms_norm(h_q, qg_ref[...])
        qt = _dot_nt(wqt_ref[...], xqn.astype(_bf16))
        qnt_ref[0, :, rows] = (qt[:HEADS * NOPE] * _SCORE_SCALE).astype(_bf16)
        half = ROPE // 2
        for h in range(HEADS):
            lo = HEADS * NOPE + h * ROPE
            t1, t2 = qt[lo:lo + half], qt[lo + half:lo + ROPE]
            roped = jnp.concatenate([t1 * qcos_t - t2 * qsin_t,
                                     t1 * qsin_t + t2 * qcos_t], axis=0)
            qrt_ref[0, h * LANES:h * LANES + ROPE, rows] = roped.astype(_bf16)
            qrt_ref[0, h * LANES + ROPE:(h + 1) * LANES, rows] = jnp.zeros((ROPE, ts), _bf16)

        xkvn = _rms_norm(h_kv, kvg_ref[...]).astype(_bf16)
        kn_ref[0, rows, :] = _dot(xkvn, wkn_ref[...]).astype(_bf16)
        vt_ref[0, :, rows] = _dot_nt(wvt_ref[...], xkvn).astype(_bf16)
        t = h_kr * tab
        kd_ref[0, rows, :] = (t + pltpu.roll(t, ROPE, axis=1)).astype(_bf16)

        ga_ref[0, rows, :] = _silu(h_ga).astype(_bf16)

        ext = jnp.concatenate([history, u], axis=0)
        sums = []
        acc = ext
        width = 1
        for g, w in enumerate(POOL_WINDOWS):
            while width < w:
                acc = acc + pltpu.roll(acc, width, axis=0)
                width *= 2
            sums.append(acc[MAX_WINDOW:, :POOL_GROUP])
            acc = acc[:, POOL_GROUP:]
        tok = si * tm + r + lax.broadcasted_iota(jnp.int32, (ts, 1), 0)
        pooled = []
        for g, w in enumerate(POOL_WINDOWS):
            inv_cnt = 1.0 / jnp.minimum(tok + 1, w).astype(_f32)
            pooled.append(sums[g] * inv_cnt - u[:, g * POOL_GROUP:(g + 1) * POOL_GROUP])
        mixed = jnp.concatenate(
            [_dot(jnp.concatenate(pooled[0:2], axis=1).astype(_bf16), wp_ref[0]),
             _dot(jnp.concatenate(pooled[2:4], axis=1).astype(_bf16), wp_ref[1])], axis=1)
        yb_ref[0, rows, :] = (mixed * ps_ref[...] * _silu(h_gb)).astype(_bf16)
        return u[ts - MAX_WINDOW:, :]

    starts = list(range(0, tm, ts))
    history = carry_ref[...]
    products = front(starts[0])
    for i, r in enumerate(starts):
        upcoming = front(starts[i + 1]) if i + 1 < len(starts) else None
        history = back(r, products, history)
        products = upcoming
    carry_ref[...] = history


def _attn_out_kernel(qstart_ref, kstart_ref, nfull_ref, nmask_ref,
                     qnt_ref, qrt_ref, kn_ref, kd_ref, vt_ref, ga_ref, pos_ref,
                     yb_ref, wo_ref, lng_ref, lnb_ref, x_hbm,
                     out_hbm,
                     m_ref, alpha_ref, acc_ref, bias_ref, xbuf_ref, obuf_ref, in_sem, out_sem,
                     *head_refs, npairs, nrows, res_scale):
    tq, tk = Q_TILE, K_TILE
    s_refs, smax_refs, p_refs = (head_refs[i * HEADS:(i + 1) * HEADS] for i in range(3))
    nq = ga_ref.shape[1] // tq
    b = pl.program_id(0)
    n_full = nfull_ref[b]
    n_all = n_full + nmask_ref[b]

    def x_copy():
        return pltpu.make_async_copy(x_hbm.at[b], xbuf_ref, in_sem.at[0])

    def out_copy():
        return pltpu.make_async_copy(obuf_ref, out_hbm.at[b], out_sem.at[0])

    x_copy().start()

    m_ref[...] = jnp.full(m_ref.shape, _MASKED, _f32)
    acc_ref[...] = jnp.zeros_like(acc_ref)

    def pair(i):
        qs = pl.multiple_of(qstart_ref[b * npairs + i], tq)
        ks = pl.multiple_of(kstart_ref[b * npairs + i], tk)
        return qs, ks

    chunks = [slice(c, c + S_CHUNK) for c in range(0, tk, S_CHUNK)]
    ones_rows = jnp.ones((L_ROWS, tk), _bf16)

    def make_bias(qs, ks):
        for c in range(0, tk, LANES):
            pk_row = pos_ref[0, :, pl.ds(ks + c, LANES)]
            pk_col = jnp.broadcast_to(pk_row, (LANES, LANES)).T
            for j in range(0, tq, LANES):
                keep = pk_col <= pos_ref[0, :, pl.ds(qs + j, LANES)]
                bias_ref[c:c + LANES, j:j + LANES] = jnp.where(keep, 0.0, _MASKED)

    def scores(h, qs, ks, masked):
        rows = slice(h * LANES, (h + 1) * LANES)
        qc = jnp.concatenate([qnt_ref[0, rows, pl.ds(qs, tq)],
                              qrt_ref[0, rows, pl.ds(qs, tq)]], axis=0)
        kc = jnp.concatenate([kn_ref[0, pl.ds(ks, tk), rows],
                              kd_ref[0, pl.ds(ks, tk), :]], axis=1)
        s = _dot(kc, qc)
        if masked:
            s = s + bias_ref[...]
        s_refs[h][...] = s
        mx = s[chunks[0], :]
        for c in chunks[1:]:
            mx = jnp.maximum(mx, s[c, :])
        smax_refs[h][...] = mx

    def softmax(h, qs):
        cols = pl.ds(qs, tq)
        sb = s_refs[h]
        m_old = m_ref[h, :, cols]
        m_new = jnp.maximum(m_old, jnp.max(smax_refs[h][...], axis=0, keepdims=True))
        alpha_ref[h] = jnp.exp2(m_old - m_new)
        m_ref[h, :, cols] = m_new
        for c in chunks:
            p_refs[h][c, :] = jnp.exp2(sb[c, :] - m_new).astype(_bf16)

    def pv(h, qs, ks):
        rows = slice(h * LANES, (h + 1) * LANES)
        cols = pl.ds(qs, tq)
        v_ext = jnp.concatenate([vt_ref[0, rows, pl.ds(ks, tk)], ones_rows], axis=0)
        acc_ref[h, :, cols] = alpha_ref[h] * acc_ref[h, :, cols] + _dot(v_ext, p_refs[h][...])

    def step(i, next_masked):
        qs, ks = pair(i)
        if next_masked is not None:
            qs_n, ks_n = pair(i + 1)
            if next_masked:
                make_bias(qs_n, ks_n)
        for h in range(HEADS):
            if next_masked is not None:
                scores(h, qs_n, ks_n, next_masked)
            pv(h, qs, ks)
            if h + 1 < HEADS:
                softmax(h + 1, qs)
            elif next_masked is not None:
                softmax(0, qs_n)

    qs0, ks0 = pair(0)

    @pl.when(n_full > 0)
    def _():
        for h in range(HEADS):
            scores(h, qs0, ks0, False)

    @pl.when(n_full == 0)
    def _():
        make_bias(qs0, ks0)
        for h in range(HEADS):
            scores(h, qs0, ks0, True)

    softmax(0, qs0)

    def step_next_full(i, carry):
        step(i, False)
        return carry

    def step_next_masked(i, carry):
        step(i, True)
        return carry

    first_masked_next = jnp.maximum(n_full - 1, 0)
    lax.fori_loop(0, first_masked_next, step_next_full, 0)
    lax.fori_loop(first_masked_next, n_all - 1, step_next_masked, 0)
    step(n_all - 1, None)

    x_copy().wait()

    @pl.when(b > 0)
    def _():
        out_copy().wait()

    for qt in range(nq):
        rows = slice(qt * tq, (qt + 1) * tq)
        ya = []
        for h in range(HEADS):
            cols = slice(h * LANES, (h + 1) * LANES)
            inv_l = 1.0 / acc_ref[h, V_DIM:V_DIM + 1, rows]
            o = (acc_ref[h, :V_DIM, rows] * inv_l).T
            ya.append((o * ga_ref[0, rows, cols].astype(_f32)).astype(_bf16))
        y = jnp.concatenate(ya + [yb_ref[0, rows, :]], axis=1)
        for r in range(0, tq, OUT_SUBTILE):
            sub = slice(qt * tq + r, qt * tq + r + OUT_SUBTILE)
            z = res_scale * xbuf_ref[sub, :] + _dot(y[r:r + OUT_SUBTILE], wo_ref[...])
            mu = jnp.mean(z, axis=-1, keepdims=True)
            zc = z - mu
            var = jnp.mean(zc * zc, axis=-1, keepdims=True)
            obuf_ref[sub, :] = zc * lax.rsqrt(var + LN_EPS) * lng_ref[...] + lnb_ref[...]
    out_copy().start()

    @pl.when(b == nrows - 1)
    def _():
        out_copy().wait()


def _rotate_half_cols(w):
    half = w.shape[-1] // 2
    return jnp.concatenate([-w[..., half:], w[..., :half]], axis=-1)


def _prepare_weights(w_in, w_uq, w_ukv, pool_w):
    w_in, w_uq, w_ukv, pool_w = (w.astype(_bf16) for w in (w_in, w_uq, w_ukv, pool_w))
    splits = np.cumsum([Q_RANK, KV_RANK, ROPE, MLA_WIDTH, POOL_WIDTH, POOL_WIDTH])[:-1]
    wq_l, wkv_l, wkr, wga, wu, wgb = jnp.split(w_in, [int(c) for c in splits], axis=1)
    w1 = jnp.concatenate([wq_l, wkv_l, wkr, _rotate_half_cols(wkr), wga, wu, wgb], axis=1)

    wq3 = w_uq.reshape(Q_RANK, HEADS, QK_DIM)
    wq_nope = wq3[:, :, :NOPE].reshape(Q_RANK, HEADS * NOPE)
    wq_rope = wq3[:, :, NOPE:].reshape(Q_RANK, HEADS * ROPE)
    wqt = jnp.concatenate([wq_nope, wq_rope], axis=1).T

    wkv3 = w_ukv.reshape(KV_RANK, HEADS, NOPE + V_DIM)
    wkn = wkv3[:, :, :NOPE].reshape(KV_RANK, HEADS * NOPE)
    wvt = wkv3[:, :, NOPE:].reshape(KV_RANK, HEADS * V_DIM).T

    z = jnp.zeros((POOL_GROUP, POOL_GROUP), pool_w.dtype)
    wp = jnp.stack([jnp.block([[pool_w[0], z], [z, pool_w[1]]]),
                    jnp.block([[pool_w[2], z], [z, pool_w[3]]])])
    return w1, wqt, wkn, wvt, wp


def _const_spec(shape):
    return pl.BlockSpec(shape, lambda *_: (0,) * len(shape), pipeline_mode=pl.Buffered(1))


def _layer(x, positions, w1, q_norm_g, wqt, kv_norm_g, wkn, wvt, wp, pool_scale, w_out,
           ln_g, ln_b):
    B, S, D = x.shape
    tm, tq, tk = TOKEN_TILE, Q_TILE, K_TILE
    ns, nq, nk = S // tm, S // tq, S // tk

    half = ROPE // 2
    inv_freq = ROPE_THETA ** (-jnp.arange(half, dtype=_f32) / half)
    freq_col = inv_freq.reshape(half, 1)
    pos_row = positions.reshape(B, 1, S)

    def tok_spec(width):
        return pl.BlockSpec((1, tm, width), lambda b, s: (b, s, 0))

    def feat_spec(width):
        return pl.BlockSpec((1, width, tm), lambda b, s: (b, 0, s))

    act = lambda width: jax.ShapeDtypeStruct((B, S, width), _bf16)
    act_t = lambda width: jax.ShapeDtypeStruct((B, width, S), _bf16)
    qnt, qrt, kn, kd, vt, ga, yb = pl.pallas_call(
        _proj_kernel,
        grid=(B, ns),
        in_specs=[tok_spec(D), feat_spec(1), _const_spec((half, 1)),
                  _const_spec(w1.shape), _const_spec((1, Q_RANK)), _const_spec(wqt.shape),
                  _const_spec((1, KV_RANK)), _const_spec(wkn.shape), _const_spec(wvt.shape),
                  _const_spec(wp.shape), _const_spec((1, POOL_WIDTH))],
        out_specs=[feat_spec(MLA_WIDTH), feat_spec(MLA_WIDTH), tok_spec(MLA_WIDTH),
                   tok_spec(LANES), feat_spec(MLA_WIDTH), tok_spec(MLA_WIDTH),
                   tok_spec(POOL_WIDTH)],
        out_shape=[act_t(MLA_WIDTH), act_t(MLA_WIDTH), act(MLA_WIDTH), act(LANES),
                   act_t(MLA_WIDTH), act(MLA_WIDTH), act(POOL_WIDTH)],
        scratch_shapes=[pltpu.VMEM((MAX_WINDOW, POOL_WIDTH), _f32)],
        compiler_params=pltpu.CompilerParams(
            dimension_semantics=("arbitrary", "arbitrary"),
            vmem_limit_bytes=VMEM_LIMIT_BYTES),
        name="mla_pool_proj",
    )(x, pos_row, freq_col, w1, q_norm_g.reshape(1, -1), wqt, kv_norm_g.reshape(1, -1), wkn,
      wvt, wp, pool_scale.reshape(1, -1))

    pq_blk = positions.reshape(B, nq, tq)
    pk_blk = positions.reshape(B, nk, tk)
    need = pq_blk.max(-1)[:, :, None] >= pk_blk.min(-1)[:, None, :]
    full = pq_blk.min(-1)[:, :, None] >= pk_blk.max(-1)[:, None, :]
    npairs = nq * nk
    rank = jnp.where(need & full, 0, jnp.where(need, 1, 2)).reshape(B, npairs)
    order = jnp.argsort(rank, axis=1, stable=True).astype(jnp.int32)
    q_start = (order // nk) * tq
    k_start = (order % nk) * tk
    n_full = jnp.sum(rank == 0, axis=1, dtype=jnp.int32)
    n_mask = jnp.sum(rank == 1, axis=1, dtype=jnp.int32)

    row_spec = lambda d1, d2: pl.BlockSpec((1, d1, d2), lambda b, *_: (b, 0, 0))
    alpha = (2.0 * DEPTH) ** 0.25
    return pl.pallas_call(
        functools.partial(_attn_out_kernel, npairs=npairs, nrows=B, res_scale=alpha),
        grid_spec=pltpu.PrefetchScalarGridSpec(
            num_scalar_prefetch=4,
            grid=(B,),
            in_specs=[row_spec(MLA_WIDTH, S), row_spec(MLA_WIDTH, S), row_spec(S, MLA_WIDTH),
                      row_spec(S, LANES), row_spec(MLA_WIDTH, S), row_spec(S, MLA_WIDTH),
                      row_spec(1, S), row_spec(S, POOL_WIDTH),
                      _const_spec(w_out.shape), _const_spec((1, D)), _const_spec((1, D)),
                      pl.BlockSpec(memory_space=pl.ANY)],
            out_specs=pl.BlockSpec(memory_space=pl.ANY),
            scratch_shapes=[pltpu.VMEM((HEADS, 1, S), _f32),
                            pltpu.VMEM((HEADS, 1, tq), _f32),
                            pltpu.VMEM((HEADS, V_DIM + L_ROWS, S), _f32),
                            pltpu.VMEM((tk, tq), _f32),
                            pltpu.VMEM((S, D), _f32),
                            pltpu.VMEM((S, D), _f32),
                            pltpu.SemaphoreType.DMA((1,)),
                            pltpu.SemaphoreType.DMA((1,))]
            + [pltpu.VMEM((tk, tq), _f32) for _ in range(HEADS)]
            + [pltpu.VMEM((S_CHUNK, tq), _f32) for _ in range(HEADS)]
            + [pltpu.VMEM((tk, tq), _bf16) for _ in range(HEADS)]),
        out_shape=jax.ShapeDtypeStruct((B, S, D), x.dtype),
        compiler_params=pltpu.CompilerParams(
            dimension_semantics=("arbitrary",),
            vmem_limit_bytes=VMEM_LIMIT_BYTES),
        name="mla_attention_out",
    )(q_start.reshape(-1), k_start.reshape(-1), n_full, n_mask,
      qnt, qrt, kn, kd, vt, ga, pos_row, yb, w_out, ln_g.reshape(1, -1), ln_b.reshape(1, -1), x)


def kernel(x, positions, w_in, q_norm_g, w_uq, kv_norm_g, w_ukv, pool_w, pool_scale, w_out,
           ln_g, ln_b):
    w1, wqt, wkn, wvt, wp = _prepare_weights(w_in, w_uq, w_ukv, pool_w)
    w_out_b = w_out.astype(_bf16)
    for layer in range(DEPTH):
        x = _layer(x, positions, w1, q_norm_g, wqt, kv_norm_g, wkn, wvt, wp, pool_scale,
                   w_out_b, ln_g[layer], ln_b[layer])
    return x
```

```python
import functools
import math

import jax
import jax.numpy as jnp
import numpy as np
from jax import lax
from jax.experimental import pallas as pl
from jax.experimental.pallas import tpu as pltpu

D_MODEL = 1024
DEPTH = 1
HEADS = 4
NOPE = 128
ROPE = 64
QK_DIM = NOPE + ROPE
V_DIM = 128
Q_RANK = 512
KV_RANK = 256
MLA_WIDTH = HEADS * V_DIM
POOL_WINDOWS = (2, 4, 8, 16)
POOL_GROUP = 128
POOL_WIDTH = len(POOL_WINDOWS) * POOL_GROUP
ROPE_THETA = 10000.0
RMS_EPS = 1e-6
LN_EPS = 1e-5

LANES = 128
MAX_WINDOW = max(POOL_WINDOWS)
VMEM_LIMIT_BYTES = 60 * 1024 * 1024

_C_Q = 0
_C_KV = _C_Q + Q_RANK
_C_KR = _C_KV + KV_RANK
_C_GA = _C_KR + 2 * ROPE
_C_U = _C_GA + MLA_WIDTH
_C_GB = _C_U + POOL_WIDTH
_C_END = _C_GB + POOL_WIDTH

_SCORE_SCALE = (QK_DIM ** -0.5) * math.log2(math.e)
_MASKED = -1e30

TOKEN_TILE = 1024
PROJ_SUBTILE = 512
OUT_SUBTILE = 256
Q_TILE = 512
K_TILE = 512
S_CHUNK = 16
L_ROWS = 16

_bf16 = jnp.bfloat16
_f32 = jnp.float32


def _dot(a, b):
    return jnp.dot(a, b, preferred_element_type=_f32)


def _dot_nt(a, b):
    return lax.dot_general(a, b, (((1,), (1,)), ((), ())), preferred_element_type=_f32)


def _rms_norm(h, g):
    ms = jnp.mean(h * h, axis=-1, keepdims=True)
    return (h * lax.rsqrt(ms + RMS_EPS)) * g


def _silu(z):
    return z * (1.0 / (1.0 + jnp.exp(-z)))


def _proj_kernel(x_ref, pos_ref, freq_ref, w1_ref, qg_ref, wqt_ref, kvg_ref, wkn_ref, wvt_ref,
                 wp_ref, ps_ref,
                 qnt_ref, qrt_ref, kn_ref, kd_ref, vt_ref, ga_ref, yb_ref,
                 carry_ref):
    tm = x_ref.shape[1]
    ts = PROJ_SUBTILE
    si = pl.program_id(1)

    @pl.when(si == 0)
    def _():
        carry_ref[...] = jnp.zeros_like(carry_ref)

    def front(r):
        rows = slice(r, r + ts)
        xb = x_ref[0, rows, :].astype(_bf16)

        ang_t = freq_ref[...] * pos_ref[0, :, rows].astype(_f32)
        cos_t, sin_t = jnp.cos(ang_t), jnp.sin(ang_t)
        tab_t = jnp.concatenate([cos_t, cos_t, sin_t, sin_t], axis=0)
        tab = tab_t.T
        qcos_t, qsin_t = cos_t * _SCORE_SCALE, sin_t * _SCORE_SCALE

        h_q = _dot(xb, w1_ref[:, _C_Q:_C_KV])
        u = _dot(xb, w1_ref[:, _C_U:_C_GB])
        h_kv = _dot(xb, w1_ref[:, _C_KV:_C_KR])
        h_kr = _dot(xb, w1_ref[:, _C_KR:_C_GA])
        h_ga = _dot(xb, w1_ref[:, _C_GA:_C_U])
        h_gb = _dot(xb, w1_ref[:, _C_GB:_C_END])
        return h_q, u, h_kv, h_kr, h_ga, h_gb, tab, qcos_t, qsin_t

    def back(r, products, history):
        rows = slice(r, r + ts)
        h_q, u, h_kv, h_kr, h_ga, h_gb, tab, qcos_t, qsin_t = products

        xqn = _rms_norm(h_q, qg_ref[...])
        qt = _dot_nt(wqt_ref[...], xqn.astype(_bf16))
        qnt_ref[0, :, rows] = (qt[:HEADS * NOPE] * _SCORE_SCALE).astype(_bf16)
        half = ROPE // 2
        for h in range(HEADS):
            lo = HEADS * NOPE + h * ROPE
            t1, t2 = qt[lo:lo + half], qt[lo + half:lo + ROPE]
            roped = jnp.concatenate([t1 * qcos_t - t2 * qsin_t,
                                     t1 * qsin_t + t2 * qcos_t], axis=0)
            qrt_ref[0, h * ROPE:(h + 1) * ROPE, rows] = roped.astype(_bf16)

        xkvn = _rms_norm(h_kv, kvg_ref[...]).astype(_bf16)
        kn_ref[0, rows, :] = _dot(xkvn, wkn_ref[...]).astype(_bf16)
        vt_ref[0, :, rows] = _dot_nt(wvt_ref[...], xkvn).astype(_bf16)
        t = h_kr * tab
        kd_ref[0, rows, :] = (t + pltpu.roll(t, ROPE, axis=1)).astype(_bf16)

        ga_ref[0, rows, :] = _silu(h_ga).astype(_bf16)

        ext = jnp.concatenate([history, u], axis=0)
        sums = []
        acc = ext
        width = 1
        for g, w in enumerate(POOL_WINDOWS):
            while width < w:
                acc = acc + pltpu.roll(acc, width, axis=0)
                width *= 2
            sums.append(acc[MAX_WINDOW:, :POOL_GROUP])
            acc = acc[:, POOL_GROUP:]
        tok = si * tm + r + lax.broadcasted_iota(jnp.int32, (ts, 1), 0)
        pooled = []
        for g, w in enumerate(POOL_WINDOWS):
            inv_cnt = 1.0 / jnp.minimum(tok + 1, w).astype(_f32)
            pooled.append(sums[g] * inv_cnt - u[:, g * POOL_GROUP:(g + 1) * POOL_GROUP])
        mixed = jnp.concatenate(
            [_dot(jnp.concatenate(pooled[0:2], axis=1).astype(_bf16), wp_ref[0]),
             _dot(jnp.concatenate(pooled[2:4], axis=1).astype(_bf16), wp_ref[1])], axis=1)
        yb_ref[0, rows, :] = (mixed * ps_ref[...] * _silu(h_gb)).astype(_bf16)
        return u[ts - MAX_WINDOW:, :]

    starts = list(range(0, tm, ts))
    history = carry_ref[...]
    products = front(starts[0])
    for i, r in enumerate(starts):
        upcoming = front(starts[i + 1]) if i + 1 < len(starts) else None
        history = back(r, products, history)
        products = upcoming
    carry_ref[...] = history


def _attn_out_kernel(qstart_ref, kstart_ref, nfull_ref, nmask_ref,
                     qnt_ref, qrt_ref, kn_ref, kd_ref, vt_ref, ga_ref, pos_ref,
                     yb_ref, wo_ref, lng_ref, lnb_ref, x_hbm,
                     out_hbm,
                     m_ref, alpha_ref, acc_ref, bias_ref, xbuf_ref, obuf_ref, in_sem, out_sem,
                     *head_refs, npairs, nrows, res_scale):
    tq, tk = Q_TILE, K_TILE
    s_refs, smax_refs, p_refs = (head_refs[i * HEADS:(i + 1) * HEADS] for i in range(3))
    nq = ga_ref.shape[1] // tq
    b = pl.program_id(0)
    n_full = nfull_ref[b]
    n_all = n_full + nmask_ref[b]

    def x_copy():
        return pltpu.make_async_copy(x_hbm.at[b], xbuf_ref, in_sem.at[0])

    def out_copy():
        return pltpu.make_async_copy(obuf_ref, out_hbm.at[b], out_sem.at[0])

    x_copy().start()

    m_ref[...] = jnp.full(m_ref.shape, _MASKED, _f32)
    acc_ref[...] = jnp.zeros_like(acc_ref)

    def pair(i):
        qs = pl.multiple_of(qstart_ref[b * npairs + i], tq)
        ks = pl.multiple_of(kstart_ref[b * npairs + i], tk)
        return qs, ks

    chunks = [slice(c, c + S_CHUNK) for c in range(0, tk, S_CHUNK)]
    ones_rows = jnp.ones((L_ROWS, tk), _bf16)

    def make_bias(qs, ks):
        for c in range(0, tk, LANES):
            pk_row = pos_ref[0, :, pl.ds(ks + c, LANES)]
            pk_col = jnp.broadcast_to(pk_row, (LANES, LANES)).T
            for j in range(0, tq, LANES):
                keep = pk_col <= pos_ref[0, :, pl.ds(qs + j, LANES)]
                bias_ref[c:c + LANES, j:j + LANES] = jnp.where(keep, 0.0, _MASKED)

    def scores(h, qs, ks, masked):
        rows = slice(h * LANES, (h + 1) * LANES)
        qc = jnp.concatenate([qnt_ref[0, rows, pl.ds(qs, tq)],
                              qrt_ref[0, h * ROPE:(h + 1) * ROPE, pl.ds(qs, tq)],
                              jnp.zeros((ROPE, tq), _bf16)], axis=0)
        kc = jnp.concatenate([kn_ref[0, pl.ds(ks, tk), rows],
                              kd_ref[0, pl.ds(ks, tk), :]], axis=1)
        s = _dot(kc, qc)
        if masked:
            s = s + bias_ref[...]
        s_refs[h][...] = s
        mx = s[chunks[0], :]
        for c in chunks[1:]:
            mx = jnp.maximum(mx, s[c, :])
        smax_refs[h][...] = mx

    def softmax(h, qs):
        cols = pl.ds(qs, tq)
        sb = s_refs[h]
        m_old = m_ref[h, :, cols]
        m_new = jnp.maximum(m_old, jnp.max(smax_refs[h][...], axis=0, keepdims=True))
        alpha_ref[h] = jnp.exp2(m_old - m_new)
        m_ref[h, :, cols] = m_new
        for c in chunks:
            p_refs[h][c, :] = jnp.exp2(sb[c, :] - m_new).astype(_bf16)

    def pv(h, qs, ks):
        rows = slice(h * LANES, (h + 1) * LANES)
        cols = pl.ds(qs, tq)
        v_ext = jnp.concatenate([vt_ref[0, rows, pl.ds(ks, tk)], ones_rows], axis=0)
        acc_ref[h, :, cols] = alpha_ref[h] * acc_ref[h, :, cols] + _dot(v_ext, p_refs[h][...])

    def step(i, next_masked):
        qs, ks = pair(i)
        if next_masked is not None:
            qs_n, ks_n = pair(i + 1)
            if next_masked:
                make_bias(qs_n, ks_n)
        for h in range(HEADS):
            if next_masked is not None:
                scores(h, qs_n, ks_n, next_masked)
            pv(h, qs, ks)
            if h + 1 < HEADS:
                softmax(h + 1, qs)
            elif next_masked is not None:
                softmax(0, qs_n)

    qs0, ks0 = pair(0)

    @pl.when(n_full > 0)
    def _():
        for h in range(HEADS):
            scores(h, qs0, ks0, False)

    @pl.when(n_full == 0)
    def _():
        make_bias(qs0, ks0)
        for h in range(HEADS):
            scores(h, qs0, ks0, True)

    softmax(0, qs0)

    def step_next_full(i, carry):
        step(i, False)
        return carry

    def step_next_masked(i, carry):
        step(i, True)
        return carry

    first_masked_next = jnp.maximum(n_full - 1, 0)
    lax.fori_loop(0, first_masked_next, step_next_full, 0)
    lax.fori_loop(first_masked_next, n_all - 1, step_next_masked, 0)
    step(n_all - 1, None)

    x_copy().wait()

    @pl.when(b > 0)
    def _():
        out_copy().wait()

    for qt in range(nq):
        rows = slice(qt * tq, (qt + 1) * tq)
        ya = []
        for h in range(HEADS):
            cols = slice(h * LANES, (h + 1) * LANES)
            inv_l = 1.0 / acc_ref[h, V_DIM:V_DIM + 1, rows]
            o = (acc_ref[h, :V_DIM, rows] * inv_l).T
            ya.append((o * ga_ref[0, rows, cols].astype(_f32)).astype(_bf16))
        y = jnp.concatenate(ya + [yb_ref[0, rows, :]], axis=1)
        for r in range(0, tq, OUT_SUBTILE):
            sub = slice(qt * tq + r, qt * tq + r + OUT_SUBTILE)
            z = res_scale * xbuf_ref[sub, :] + _dot(y[r:r + OUT_SUBTILE], wo_ref[...])
            mu = jnp.mean(z, axis=-1, keepdims=True)
            zc = z - mu
            var = jnp.mean(zc * zc, axis=-1, keepdims=True)
            obuf_ref[sub, :] = zc * lax.rsqrt(var + LN_EPS) * lng_ref[...] + lnb_ref[...]
    out_copy().start()

    @pl.when(b == nrows - 1)
    def _():
        out_copy().wait()


def _rotate_half_cols(w):
    half = w.shape[-1] // 2
    return jnp.concatenate([-w[..., half:], w[..., :half]], axis=-1)


def _prepare_weights(w_in, w_uq, w_ukv, pool_w):
    w_in, w_uq, w_ukv, pool_w = (w.astype(_bf16) for w in (w_in, w_uq, w_ukv, pool_w))
    splits = np.cumsum([Q_RANK, KV_RANK, ROPE, MLA_WIDTH, POOL_WIDTH, POOL_WIDTH])[:-1]
    wq_l, wkv_l, wkr, wga, wu, wgb = jnp.split(w_in, [int(c) for c in splits], axis=1)
    w1 = jnp.concatenate([wq_l, wkv_l, wkr, _rotate_half_cols(wkr), wga, wu, wgb], axis=1)

    wq3 = w_uq.reshape(Q_RANK, HEADS, QK_DIM)
    wq_nope = wq3[:, :, :NOPE].reshape(Q_RANK, HEADS * NOPE)
    wq_rope = wq3[:, :, NOPE:].reshape(Q_RANK, HEADS * ROPE)
    wqt = jnp.concatenate([wq_nope, wq_rope], axis=1).T

    wkv3 = w_ukv.reshape(KV_RANK, HEADS, NOPE + V_DIM)
    wkn = wkv3[:, :, :NOPE].reshape(KV_RANK, HEADS * NOPE)
    wvt = wkv3[:, :, NOPE:].reshape(KV_RANK, HEADS * V_DIM).T

    z = jnp.zeros((POOL_GROUP, POOL_GROUP), pool_w.dtype)
    wp = jnp.stack([jnp.block([[pool_w[0], z], [z, pool_w[1]]]),
                    jnp.block([[pool_w[2], z], [z, pool_w[3]]])])
    return w1, wqt, wkn, wvt, wp


def _const_spec(shape):
    return pl.BlockSpec(shape, lambda *_: (0,) * len(shape), pipeline_mode=pl.Buffered(1))


def _layer(x, positions, w1, q_norm_g, wqt, kv_norm_g, wkn, wvt, wp, pool_scale, w_out,
           ln_g, ln_b):
    B, S, D = x.shape
    tm, tq, tk = TOKEN_TILE, Q_TILE, K_TILE
    ns, nq, nk = S // tm, S // tq, S // tk

    half = ROPE // 2
    inv_freq = ROPE_THETA ** (-jnp.arange(half, dtype=_f32) / half)
    freq_col = inv_freq.reshape(half, 1)
    pos_row = positions.reshape(B, 1, S)

    def tok_spec(width):
        return pl.BlockSpec((1, tm, width), lambda b, s: (b, s, 0))

    def feat_spec(width):
        return pl.BlockSpec((1, width, tm), lambda b, s: (b, 0, s))

    act = lambda width: jax.ShapeDtypeStruct((B, S, width), _bf16)
    act_t = lambda width: jax.ShapeDtypeStruct((B, width, S), _bf16)
    qnt, qrt, kn, kd, vt, ga, yb = pl.pallas_call(
        _proj_kernel,
        grid=(B, ns),
        in_specs=[tok_spec(D), feat_spec(1), _const_spec((half, 1)),
                  _const_spec(w1.shape), _const_spec((1, Q_RANK)), _const_spec(wqt.shape),
                  _const_spec((1, KV_RANK)), _const_spec(wkn.shape), _const_spec(wvt.shape),
                  _const_spec(wp.shape), _const_spec((1, POOL_WIDTH))],
        out_specs=[feat_spec(MLA_WIDTH), feat_spec(HEADS * ROPE), tok_spec(MLA_WIDTH),
                   tok_spec(LANES), feat_spec(MLA_WIDTH), tok_spec(MLA_WIDTH),
                   tok_spec(POOL_WIDTH)],
        out_shape=[act_t(MLA_WIDTH), act_t(HEADS * ROPE), act(MLA_WIDTH), act(LANES),
                   act_t(MLA_WIDTH), act(MLA_WIDTH), act(POOL_WIDTH)],
        scratch_shapes=[pltpu.VMEM((MAX_WINDOW, POOL_WIDTH), _f32)],
        compiler_params=pltpu.CompilerParams(
            dimension_semantics=("arbitrary", "arbitrary"),
            vmem_limit_bytes=VMEM_LIMIT_BYTES),
        name="mla_pool_proj",
    )(x, pos_row, freq_col, w1, q_norm_g.reshape(1, -1), wqt, kv_norm_g.reshape(1, -1), wkn,
      wvt, wp, pool_scale.reshape(1, -1))

    pq_blk = positions.reshape(B, nq, tq)
    pk_blk = positions.reshape(B, nk, tk)
    need = pq_blk.max(-1)[:, :, None] >= pk_blk.min(-1)[:, None, :]
    full = pq_blk.min(-1)[:, :, None] >= pk_blk.max(-1)[:, None, :]
    npairs = nq * nk
    rank = jnp.where(need & full, 0, jnp.where(need, 1, 2)).reshape(B, npairs)
    order = jnp.argsort(rank, axis=1, stable=True).astype(jnp.int32)
    q_start = (order // nk) * tq
    k_start = (order % nk) * tk
    n_full = jnp.sum(rank == 0, axis=1, dtype=jnp.int32)
    n_mask = jnp.sum(rank == 1, axis=1, dtype=jnp.int32)

    row_spec = lambda d1, d2: pl.BlockSpec((1, d1, d2), lambda b, *_: (b, 0, 0))
    alpha = (2.0 * DEPTH) ** 0.25
    return pl.pallas_call(
        functools.partial(_attn_out_kernel, npairs=npairs, nrows=B, res_scale=alpha),
        grid_spec=pltpu.PrefetchScalarGridSpec(
            num_scalar_prefetch=4,
            grid=(B,),
            in_specs=[row_spec(MLA_WIDTH, S), row_spec(HEADS * ROPE, S), row_spec(S, MLA_WIDTH),
                      row_spec(S, LANES), row_spec(MLA_WIDTH, S), row_spec(S, MLA_WIDTH),
                      row_spec(1, S), row_spec(S, POOL_WIDTH),
                      _const_spec(w_out.shape), _const_spec((1, D)), _const_spec((1, D)),
                      pl.BlockSpec(memory_space=pl.ANY)],
            out_specs=pl.BlockSpec(memory_space=pl.ANY),
            scratch_shapes=[pltpu.VMEM((HEADS, 1, S), _f32),
                            pltpu.VMEM((HEADS, 1, tq), _f32),
                            pltpu.VMEM((HEADS, V_DIM + L_ROWS, S), _f32),
                            pltpu.VMEM((tk, tq), _f32),
                            pltpu.VMEM((S, D), _f32),
                            pltpu.VMEM((S, D), _f32),
                            pltpu.SemaphoreType.DMA((1,)),
                            pltpu.SemaphoreType.DMA((1,))]
            + [pltpu.VMEM((tk, tq), _f32) for _ in range(HEADS)]
            + [pltpu.VMEM((S_CHUNK, tq), _f32) for _ in range(HEADS)]
            + [pltpu.VMEM((tk, tq), _bf16) for _ in range(HEADS)]),
        out_shape=jax.ShapeDtypeStruct((B, S, D), x.dtype),
        compiler_params=pltpu.CompilerParams(
            dimension_semantics=("arbitrary",),
            vmem_limit_bytes=VMEM_LIMIT_BYTES),
        name="mla_attention_out",
    )(q_start.reshape(-1), k_start.reshape(-1), n_full, n_mask,
      qnt, qrt, kn, kd, vt, ga, pos_row, yb, w_out, ln_g.reshape(1, -1), ln_b.reshape(1, -1), x)


def kernel(x, positions, w_in, q_norm_g, w_uq, kv_norm_g, w_ukv, pool_w, pool_scale, w_out,
           ln_g, ln_b):
    w1, wqt, wkn, wvt, wp = _prepare_weights(w_in, w_uq, w_ukv, pool_w)
    w_out_b = w_out.astype(_bf16)
    for layer in range(DEPTH):
        x = _layer(x, positions, w1, q_norm_g, wqt, kv_norm_g, wkn, wvt, wp, pool_scale,
                   w_out_b, ln_g[layer], ln_b[layer])
    return x
```

```python
import functools
import math

import jax
import jax.numpy as jnp
import numpy as np
from jax import lax
from jax.experimental import pallas as pl
from jax.experimental.pallas import tpu as pltpu

D_MODEL = 1024
DEPTH = 1
HEADS = 4
NOPE = 128
ROPE = 64
QK_DIM = NOPE + ROPE
V_DIM = 128
Q_RANK = 512
KV_RANK = 256
MLA_WIDTH = HEADS * V_DIM
POOL_WINDOWS = (2, 4, 8, 16)
POOL_GROUP = 128
POOL_WIDTH = len(POOL_WINDOWS) * POOL_GROUP
ROPE_THETA = 10000.0
RMS_EPS = 1e-6
LN_EPS = 1e-5

LANES = 128
MAX_WINDOW = max(POOL_WINDOWS)
VMEM_LIMIT_BYTES = 60 * 1024 * 1024

_C_Q = 0
_C_KV = _C_Q + Q_RANK
_C_KR = _C_KV + KV_RANK
_C_GA = _C_KR + 2 * ROPE
_C_U = _C_GA + MLA_WIDTH
_C_GB = _C_U + POOL_WIDTH
_C_END = _C_GB + POOL_WIDTH

_SCORE_SCALE = (QK_DIM ** -0.5) * math.log2(math.e)
_MASKED = -1e30

TOKEN_TILE = 1024
PROJ_SUBTILE = 512
OUT_SUBTILE = 256
Q_TILE = 512
K_TILE = 512
S_CHUNK = 16
L_ROWS = 16

_bf16 = jnp.bfloat16
_f32 = jnp.float32


def _dot(a, b):
    return jnp.dot(a, b, preferred_element_type=_f32)


def _dot_nt(a, b):
    return lax.dot_general(a, b, (((1,), (1,)), ((), ())), preferred_element_type=_f32)


def _rms_norm(h, g):
    ms = jnp.mean(h * h, axis=-1, keepdims=True)
    return (h * lax.rsqrt(ms + RMS_EPS)) * g


def _silu(z):
    return z * (1.0 / (1.0 + jnp.exp(-z)))


def _proj_kernel(x_ref, pos_ref, freq_ref, w1_ref, qg_ref, wqt_ref, kvg_ref, wkn_ref, wvt_ref,
                 wp_ref, ps_ref,
                 qnt_ref, qrt_ref, kn_ref, kd_ref, vt_ref, ga_ref, yb_ref,
                 carry_ref):
    tm = x_ref.shape[1]
    ts = PROJ_SUBTILE
    si = pl.program_id(1)

    @pl.when(si == 0)
    def _():
        carry_ref[...] = jnp.zeros_like(carry_ref)

    def front(r):
        rows = slice(r, r + ts)
        xb = x_ref[0, rows, :].astype(_bf16)

        ang_t = freq_ref[...] * pos_ref[0, :, rows].astype(_f32)
        cos_t, sin_t = jnp.cos(ang_t), jnp.sin(ang_t)
        tab_t = jnp.concatenate([cos_t, cos_t, sin_t, sin_t], axis=0)
        tab = tab_t.T
        qcos_t, qsin_t = cos_t * _SCORE_SCALE, sin_t * _SCORE_SCALE

        h_q = _dot(xb, w1_ref[:, _C_Q:_C_KV])
        u = _dot(xb, w1_ref[:, _C_U:_C_GB])
        h_kv = _dot(xb, w1_ref[:, _C_KV:_C_KR])
        h_kr = _dot(xb, w1_ref[:, _C_KR:_C_GA])
        h_ga = _dot(xb, w1_ref[:, _C_GA:_C_U])
        h_gb = _dot(xb, w1_ref[:, _C_GB:_C_END])
        return h_q, u, h_kv, h_kr, h_ga, h_gb, tab, qcos_t, qsin_t

    def back(r, products, history):
        rows = slice(r, r + ts)
        h_q, u, h_kv, h_kr, h_ga, h_gb, tab, qcos_t, qsin_t = products

        xqn = _rms_norm(h_q, qg_ref[...])
        qt = _dot_nt(wqt_ref[...], xqn.astype(_bf16))
        qnt_ref[0, :, rows] = (qt[:HEADS * NOPE] * _SCORE_SCALE).astype(_bf16)
        half = ROPE // 2
        for h in range(HEADS):
            lo = HEADS * NOPE + h * ROPE
            t1, t2 = qt[lo:lo + half], qt[lo + half:lo + ROPE]
            roped = jnp.concatenate([t1 * qcos_t - t2 * qsin_t,
                                     t1 * qsin_t + t2 * qcos_t], axis=0)
            qrt_ref[0, h * ROPE:(h + 1) * ROPE, rows] = roped.astype(_bf16)

        xkvn = _rms_norm(h_kv, kvg_ref[...]).astype(_bf16)
        kn_ref[0, rows, :] = _dot(xkvn, wkn_ref[...]).astype(_bf16)
        vt_ref[0, :, rows] = _dot_nt(wvt_ref[...], xkvn).astype(_bf16)
        t = h_kr * tab
        kd_ref[0, rows, :] = (t + pltpu.roll(t, ROPE, axis=1)).astype(_bf16)

        ga_ref[0, rows, :] = _silu(h_ga).astype(_bf16)

        ext = jnp.concatenate([history, u], axis=0)
        sums = []
        acc = ext
        width = 1
        for g, w in enumerate(POOL_WINDOWS):
            while width < w:
                acc = acc + pltpu.roll(acc, width, axis=0)
                width *= 2
            sums.append(acc[MAX_WINDOW:, :POOL_GROUP])
            acc = acc[:, POOL_GROUP:]
        tok = si * tm + r + lax.broadcasted_iota(jnp.int32, (ts, 1), 0)
        pooled = []
        for g, w in enumerate(POOL_WINDOWS):
            inv_cnt = 1.0 / jnp.minimum(tok + 1, w).astype(_f32)
            pooled.append(sums[g] * inv_cnt - u[:, g * POOL_GROUP:(g + 1) * POOL_GROUP])
        mixed = jnp.concatenate(
            [_dot(jnp.concatenate(pooled[0:2], axis=1).astype(_bf16), wp_ref[0]),
             _dot(jnp.concatenate(pooled[2:4], axis=1).astype(_bf16), wp_ref[1])], axis=1)
        yb_ref[0, rows, :] = (mixed * ps_ref[...] * _silu(h_gb)).astype(_bf16)
        return u[ts - MAX_WINDOW:, :]

    starts = list(range(0, tm, ts))
    history = carry_ref[...]
    products = front(starts[0])
    for i, r in enumerate(starts):
        upcoming = front(starts[i + 1]) if i + 1 < len(starts) else None
        history = back(r, products, history)
        products = upcoming
    carry_ref[...] = history


def _attn_out_kernel(qstart_ref, kstart_ref, nfull_ref, nmask_ref,
                     qnt_ref, qrt_ref, kn_ref, kd_ref, vt_ref, ga_ref, pos_ref,
                     yb_ref, wo_ref, lng_ref, lnb_ref, x_hbm,
                     out_hbm,
                     m_ref, alpha_ref, acc_ref, bias_ref, xbuf_ref, obuf_ref, in_sem, out_sem,
                     *head_refs, npairs, nrows, res_scale):
    tq, tk = Q_TILE, K_TILE
    s_refs, smax_refs, p_refs = (head_refs[i * HEADS:(i + 1) * HEADS] for i in range(3))
    nq = ga_ref.shape[1] // tq
    b = pl.program_id(0)
    n_full = nfull_ref[b]
    n_all = n_full + nmask_ref[b]

    def x_copy():
        return pltpu.make_async_copy(x_hbm.at[b], xbuf_ref, in_sem.at[0])

    def out_copy():
        return pltpu.make_async_copy(obuf_ref, out_hbm.at[b], out_sem.at[0])

    x_copy().start()

    m_ref[...] = jnp.full(m_ref.shape, _MASKED, _f32)

    @pl.when(b == 0)
    def _():
        acc_ref[...] = jnp.zeros_like(acc_ref)

    def pair(i):
        qs = pl.multiple_of(qstart_ref[b * npairs + i], tq)
        ks = pl.multiple_of(kstart_ref[b * npairs + i], tk)
        return qs, ks

    chunks = [slice(c, c + S_CHUNK) for c in range(0, tk, S_CHUNK)]
    ones_rows = jnp.ones((L_ROWS, tk), _bf16)

    def make_bias(qs, ks):
        for c in range(0, tk, LANES):
            pk_row = pos_ref[0, :, pl.ds(ks + c, LANES)]
            pk_col = jnp.broadcast_to(pk_row, (LANES, LANES)).T
            for j in range(0, tq, LANES):
                keep = pk_col <= pos_ref[0, :, pl.ds(qs + j, LANES)]
                bias_ref[c:c + LANES, j:j + LANES] = jnp.where(keep, 0.0, _MASKED)

    def scores(h, qs, ks, masked):
        rows = slice(h * LANES, (h + 1) * LANES)
        qc = jnp.concatenate([qnt_ref[0, rows, pl.ds(qs, tq)],
                              qrt_ref[0, h * ROPE:(h + 1) * ROPE, pl.ds(qs, tq)],
                              jnp.zeros((ROPE, tq), _bf16)], axis=0)
        kc = jnp.concatenate([kn_ref[0, pl.ds(ks, tk), rows],
                              kd_ref[0, pl.ds(ks, tk), :]], axis=1)
        s = _dot(kc, qc)
        if masked:
            s = s + bias_ref[...]
        s_refs[h][...] = s
        mx = s[chunks[0], :]
        for c in chunks[1:]:
            mx = jnp.maximum(mx, s[c, :])
        smax_refs[h][...] = mx

    def softmax(h, qs):
        cols = pl.ds(qs, tq)
        sb = s_refs[h]
        m_old = m_ref[h, :, cols]
        m_new = jnp.maximum(m_old, jnp.max(smax_refs[h][...], axis=0, keepdims=True))
        alpha_ref[h] = jnp.exp2(m_old - m_new)
        m_ref[h, :, cols] = m_new
        for c in chunks:
            p_refs[h][c, :] = jnp.exp2(sb[c, :] - m_new).astype(_bf16)

    def pv(h, qs, ks):
        rows = slice(h * LANES, (h + 1) * LANES)
        cols = pl.ds(qs, tq)
        v_ext = jnp.concatenate([vt_ref[0, rows, pl.ds(ks, tk)], ones_rows], axis=0)
        acc_ref[h, :, cols] = alpha_ref[h] * acc_ref[h, :, cols] + _dot(v_ext, p_refs[h][...])

    def step(i, next_masked):
        qs, ks = pair(i)
        if next_masked is not None:
            qs_n, ks_n = pair(i + 1)
            if next_masked:
                make_bias(qs_n, ks_n)
        for h in range(HEADS):
            if next_masked is not None:
                scores(h, qs_n, ks_n, next_masked)
            pv(h, qs, ks)
            if h + 1 < HEADS:
                softmax(h + 1, qs)
            elif next_masked is not None:
                softmax(0, qs_n)

    qs0, ks0 = pair(0)

    @pl.when(n_full > 0)
    def _():
        for h in range(HEADS):
            scores(h, qs0, ks0, False)

    @pl.when(n_full == 0)
    def _():
        make_bias(qs0, ks0)
        for h in range(HEADS):
            scores(h, qs0, ks0, True)

    softmax(0, qs0)

    def step_next_full(i, carry):
        step(i, False)
        return carry

    def step_next_masked(i, carry):
        step(i, True)
        return carry

    first_masked_next = jnp.maximum(n_full - 1, 0)
    lax.fori_loop(0, first_masked_next, step_next_full, 0)
    lax.fori_loop(first_masked_next, n_all - 1, step_next_masked, 0)
    step(n_all - 1, None)

    x_copy().wait()

    @pl.when(b > 0)
    def _():
        out_copy().wait()

    for qt in range(nq):
        rows = slice(qt * tq, (qt + 1) * tq)
        ya = []
        for h in range(HEADS):
            cols = slice(h * LANES, (h + 1) * LANES)
            inv_l = 1.0 / acc_ref[h, V_DIM:V_DIM + 1, rows]
            o = (acc_ref[h, :V_DIM, rows] * inv_l).T
            ya.append((o * ga_ref[0, rows, cols].astype(_f32)).astype(_bf16))
        y = jnp.concatenate(ya + [yb_ref[0, rows, :]], axis=1)
        for r in range(0, tq, OUT_SUBTILE):
            sub = slice(qt * tq + r, qt * tq + r + OUT_SUBTILE)
            z = res_scale * xbuf_ref[sub, :] + _dot(y[r:r + OUT_SUBTILE], wo_ref[...])
            mu = jnp.mean(z, axis=-1, keepdims=True)
            zc = z - mu
            var = jnp.mean(zc * zc, axis=-1, keepdims=True)
            obuf_ref[sub, :] = zc * lax.rsqrt(var + LN_EPS) * lng_ref[...] + lnb_ref[...]
    out_copy().start()

    @pl.when(b == nrows - 1)
    def _():
        out_copy().wait()


def _rotate_half_cols(w):
    half = w.shape[-1] // 2
    return jnp.concatenate([-w[..., half:], w[..., :half]], axis=-1)


def _prepare_weights(w_in, w_uq, w_ukv, pool_w):
    w_in, w_uq, w_ukv, pool_w = (w.astype(_bf16) for w in (w_in, w_uq, w_ukv, pool_w))
    splits = np.cumsum([Q_RANK, KV_RANK, ROPE, MLA_WIDTH, POOL_WIDTH, POOL_WIDTH])[:-1]
    wq_l, wkv_l, wkr, wga, wu, wgb = jnp.split(w_in, [int(c) for c in splits], axis=1)
    w1 = jnp.concatenate([wq_l, wkv_l, wkr, _rotate_half_cols(wkr), wga, wu, wgb], axis=1)

    wq3 = w_uq.reshape(Q_RANK, HEADS, QK_DIM)
    wq_nope = wq3[:, :, :NOPE].reshape(Q_RANK, HEADS * NOPE)
    wq_rope = wq3[:, :, NOPE:].reshape(Q_RANK, HEADS * ROPE)
    wqt = jnp.concatenate([wq_nope, wq_rope], axis=1).T

    wkv3 = w_ukv.reshape(KV_RANK, HEADS, NOPE + V_DIM)
    wkn = wkv3[:, :, :NOPE].reshape(KV_RANK, HEADS * NOPE)
    wvt = wkv3[:, :, NOPE:].reshape(KV_RANK, HEADS * V_DIM).T

    z = jnp.zeros((POOL_GROUP, POOL_GROUP), pool_w.dtype)
    wp = jnp.stack([jnp.block([[pool_w[0], z], [z, pool_w[1]]]),
                    jnp.block([[pool_w[2], z], [z, pool_w[3]]])])
    return w1, wqt, wkn, wvt, wp


def _const_spec(shape):
    return pl.BlockSpec(shape, lambda *_: (0,) * len(shape), pipeline_mode=pl.Buffered(1))


def _layer(x, positions, w1, q_norm_g, wqt, kv_norm_g, wkn, wvt, wp, pool_scale, w_out,
           ln_g, ln_b):
    B, S, D = x.shape
    tm, tq, tk = TOKEN_TILE, Q_TILE, K_TILE
    ns, nq, nk = S // tm, S // tq, S // tk

    half = ROPE // 2
    inv_freq = ROPE_THETA ** (-jnp.arange(half, dtype=_f32) / half)
    freq_col = inv_freq.reshape(half, 1)
    pos_row = positions.reshape(B, 1, S)

    def tok_spec(width):
        return pl.BlockSpec((1, tm, width), lambda b, s: (b, s, 0))

    def feat_spec(width):
        return pl.BlockSpec((1, width, tm), lambda b, s: (b, 0, s))

    act = lambda width: jax.ShapeDtypeStruct((B, S, width), _bf16)
    act_t = lambda width: jax.ShapeDtypeStruct((B, width, S), _bf16)
    qnt, qrt, kn, kd, vt, ga, yb = pl.pallas_call(
        _proj_kernel,
        grid=(B, ns),
        in_specs=[tok_spec(D), feat_spec(1), _const_spec((half, 1)),
                  _const_spec(w1.shape), _const_spec((1, Q_RANK)), _const_spec(wqt.shape),
                  _const_spec((1, KV_RANK)), _const_spec(wkn.shape), _const_spec(wvt.shape),
                  _const_spec(wp.shape), _const_spec((1, POOL_WIDTH))],
        out_specs=[feat_spec(MLA_WIDTH), feat_spec(HEADS * ROPE), tok_spec(MLA_WIDTH),
                   tok_spec(LANES), feat_spec(MLA_WIDTH), tok_spec(MLA_WIDTH),
                   tok_spec(POOL_WIDTH)],
        out_shape=[act_t(MLA_WIDTH), act_t(HEADS * ROPE), act(MLA_WIDTH), act(LANES),
                   act_t(MLA_WIDTH), act(MLA_WIDTH), act(POOL_WIDTH)],
        scratch_shapes=[pltpu.VMEM((MAX_WINDOW, POOL_WIDTH), _f32)],
        compiler_params=pltpu.CompilerParams(
            dimension_semantics=("arbitrary", "arbitrary"),
            vmem_limit_bytes=VMEM_LIMIT_BYTES),
        name="mla_pool_proj",
    )(x, pos_row, freq_col, w1, q_norm_g.reshape(1, -1), wqt, kv_norm_g.reshape(1, -1), wkn,
      wvt, wp, pool_scale.reshape(1, -1))

    pq_blk = positions.reshape(B, nq, tq)
    pk_blk = positions.reshape(B, nk, tk)
    need = pq_blk.max(-1)[:, :, None] >= pk_blk.min(-1)[:, None, :]
    full = pq_blk.min(-1)[:, :, None] >= pk_blk.max(-1)[:, None, :]
    npairs = nq * nk
    rank = jnp.where(need & full, 0, jnp.where(need, 1, 2)).reshape(B, npairs)
    order = jnp.argsort(rank, axis=1, stable=True).astype(jnp.int32)
    q_start = (order // nk) * tq
    k_start = (order % nk) * tk
    n_full = jnp.sum(rank == 0, axis=1, dtype=jnp.int32)
    n_mask = jnp.sum(rank == 1, axis=1, dtype=jnp.int32)

    row_spec = lambda d1, d2: pl.BlockSpec((1, d1, d2), lambda b, *_: (b, 0, 0))
    alpha = (2.0 * DEPTH) ** 0.25
    return pl.pallas_call(
        functools.partial(_attn_out_kernel, npairs=npairs, nrows=B, res_scale=alpha),
        grid_spec=pltpu.PrefetchScalarGridSpec(
            num_scalar_prefetch=4,
            grid=(B,),
            in_specs=[row_spec(MLA_WIDTH, S), row_spec(HEADS * ROPE, S), row_spec(S, MLA_WIDTH),
                      row_spec(S, LANES), row_spec(MLA_WIDTH, S), row_spec(S, MLA_WIDTH),
                      row_spec(1, S), row_spec(S, POOL_WIDTH),
                      _const_spec(w_out.shape), _const_spec((1, D)), _const_spec((1, D)),
                      pl.BlockSpec(memory_space=pl.ANY)],
            out_specs=pl.BlockSpec(memory_space=pl.ANY),
            scratch_shapes=[pltpu.VMEM((HEADS, 1, S), _f32),
                            pltpu.VMEM((HEADS, 1, tq), _f32),
                            pltpu.VMEM((HEADS, V_DIM + L_ROWS, S), _f32),
                            pltpu.VMEM((tk, tq), _f32),
                            pltpu.VMEM((S, D), _f32),
                            pltpu.VMEM((S, D), _f32),
                            pltpu.SemaphoreType.DMA((1,)),
                            pltpu.SemaphoreType.DMA((1,))]
            + [pltpu.VMEM((tk, tq), _f32) for _ in range(HEADS)]
            + [pltpu.VMEM((S_CHUNK, tq), _f32) for _ in range(HEADS)]
            + [pltpu.VMEM((tk, tq), _bf16) for _ in range(HEADS)]),
        out_shape=jax.ShapeDtypeStruct((B, S, D), x.dtype),
        compiler_params=pltpu.CompilerParams(
            dimension_semantics=("arbitrary",),
            vmem_limit_bytes=VMEM_LIMIT_BYTES),
        name="mla_attention_out",
    )(q_start.reshape(-1), k_start.reshape(-1), n_full, n_mask,
      qnt, qrt, kn, kd, vt, ga, pos_row, yb, w_out, ln_g.reshape(1, -1), ln_b.reshape(1, -1), x)


def kernel(x, positions, w_in, q_norm_g, w_uq, kv_norm_g, w_ukv, pool_w, pool_scale, w_out,
           ln_g, ln_b):
    w1, wqt, wkn, wvt, wp = _prepare_weights(w_in, w_uq, w_ukv, pool_w)
    w_out_b = w_out.astype(_bf16)
    for layer in range(DEPTH):
        x = _layer(x, positions, w1, q_norm_g, wqt, kv_norm_g, wkn, wvt, wp, pool_scale,
                   w_out_b, ln_g[layer], ln_b[layer])
    return x
```

```python
import functools
import math

import jax
import jax.numpy as jnp
import numpy as np
from jax import lax
from jax.experimental import pallas as pl
from jax.experimental.pallas import tpu as pltpu

D_MODEL = 1024
DEPTH = 1
HEADS = 4
NOPE = 128
ROPE = 64
QK_DIM = NOPE + ROPE
V_DIM = 128
Q_RANK = 512
KV_RANK = 256
MLA_WIDTH = HEADS * V_DIM
POOL_WINDOWS = (2, 4, 8, 16)
POOL_GROUP = 128
POOL_WIDTH = len(POOL_WINDOWS) * POOL_GROUP
ROPE_THETA = 10000.0
RMS_EPS = 1e-6
LN_EPS = 1e-5

LANES = 128
MAX_WINDOW = max(POOL_WINDOWS)
VMEM_LIMIT_BYTES = 60 * 1024 * 1024

_C_Q = 0
_C_KV = _C_Q + Q_RANK
_C_KR = _C_KV + KV_RANK
_C_GA = _C_KR + 2 * ROPE
_C_U = _C_GA + MLA_WIDTH
_C_GB = _C_U + POOL_WIDTH
_C_END = _C_GB + POOL_WIDTH

_SCORE_SCALE = (QK_DIM ** -0.5) * math.log2(math.e)
_MASKED = -1e30

TOKEN_TILE = 1024
PROJ_SUBTILE = 512
OUT_SUBTILE = 256
Q_TILE = 512
K_TILE = 512
S_CHUNK = 16
L_ROWS = 16

_bf16 = jnp.bfloat16
_f32 = jnp.float32


def _dot(a, b):
    return jnp.dot(a, b, preferred_element_type=_f32)


def _dot_nt(a, b):
    return lax.dot_general(a, b, (((1,), (1,)), ((), ())), preferred_element_type=_f32)


def _rms_norm(h, g):
    ms = jnp.mean(h * h, axis=-1, keepdims=True)
    return (h * lax.rsqrt(ms + RMS_EPS)) * g


def _silu(z):
    return z * (1.0 / (1.0 + jnp.exp(-z)))


def _proj_kernel(x_ref, pos_ref, freq_ref, w1_ref, qg_ref, wqt_ref, kvg_ref, wkn_ref, wvt_ref,
                 wp_ref, ps_ref,
                 qnt_ref, qrt_ref, kn_ref, kd_ref, vt_ref, ga_ref, yb_ref,
                 carry_ref):
    tm = x_ref.shape[1]
    ts = PROJ_SUBTILE
    si = pl.program_id(1)

    @pl.when(si == 0)
    def _():
        carry_ref[...] = jnp.zeros_like(carry_ref)

    def front(r):
        rows = slice(r, r + ts)
        xb = x_ref[0, rows, :].astype(_bf16)

        ang_t = freq_ref[...] * pos_ref[0, :, rows].astype(_f32)
        cos_t, sin_t = jnp.cos(ang_t), jnp.sin(ang_t)
        tab_t = jnp.concatenate([cos_t, cos_t, sin_t, sin_t], axis=0)
        tab = tab_t.T
        qcos_t, qsin_t = cos_t * _SCORE_SCALE, sin_t * _SCORE_SCALE

        h_q = _dot(xb, w1_ref[:, _C_Q:_C_KV])
        u = _dot(xb, w1_ref[:, _C_U:_C_GB])
        h_kv = _dot(xb, w1_ref[:, _C_KV:_C_KR])
        h_kr = _dot(xb, w1_ref[:, _C_KR:_C_GA])
        h_ga = _dot(xb, w1_ref[:, _C_GA:_C_U])
        h_gb = _dot(xb, w1_ref[:, _C_GB:_C_END])
        return h_q, u, h_kv, h_kr, h_ga, h_gb, tab, qcos_t, qsin_t

    def back(r, products, history):
        rows = slice(r, r + ts)
        h_q, u, h_kv, h_kr, h_ga, h_gb, tab, qcos_t, qsin_t = products

        xqn = _rms_norm(h_q, qg_ref[...])
        qt = _dot_nt(wqt_ref[...], xqn.astype(_bf16))
        qnt_ref[0, :, rows] = (qt[:HEADS * NOPE] * _SCORE_SCALE).astype(_bf16)
        half = ROPE // 2
        for h in range(HEADS):
            lo = HEADS * NOPE + h * ROPE
            t1, t2 = qt[lo:lo + half], qt[lo + half:lo + ROPE]
            roped = jnp.concatenate([t1 * qcos_t - t2 * qsin_t,
                                     t1 * qsin_t + t2 * qcos_t], axis=0)
            qrt_ref[0, h * ROPE:(h + 1) * ROPE, rows] = roped.astype(_bf16)

        xkvn = _rms_norm(h_kv, kvg_ref[...]).astype(_bf16)
        kn_ref[0, rows, :] = _dot(xkvn, wkn_ref[...]).astype(_bf16)
        vt_ref[0, :, rows] = _dot_nt(wvt_ref[...], xkvn).astype(_bf16)
        t = h_kr * tab
        kd_ref[0, rows, :] = (t + pltpu.roll(t, ROPE, axis=1)).astype(_bf16)

        ga_ref[0, rows, :] = _silu(h_ga).astype(_bf16)

        ext = jnp.concatenate([history, u], axis=0)
        sums = []
        acc = ext
        width = 1
        for g, w in enumerate(POOL_WINDOWS):
            while width < w:
                acc = acc + pltpu.roll(acc, width, axis=0)
                width *= 2
            sums.append(acc[MAX_WINDOW:, :POOL_GROUP])
            acc = acc[:, POOL_GROUP:]
        tok = si * tm + r + lax.broadcasted_iota(jnp.int32, (ts, 1), 0)
        pooled = []
        for g, w in enumerate(POOL_WINDOWS):
            inv_cnt = 1.0 / jnp.minimum(tok + 1, w).astype(_f32)
            pooled.append(sums[g] * inv_cnt - u[:, g * POOL_GROUP:(g + 1) * POOL_GROUP])
        mixed = jnp.concatenate(
            [_dot(jnp.concatenate(pooled[0:2], axis=1).astype(_bf16), wp_ref[0]),
             _dot(jnp.concatenate(pooled[2:4], axis=1).astype(_bf16), wp_ref[1])], axis=1)
        yb_ref[0, rows, :] = (mixed * ps_ref[...] * _silu(h_gb)).astype(_bf16)
        return u[ts - MAX_WINDOW:, :]

    starts = list(range(0, tm, ts))
    history = carry_ref[...]
    products = front(starts[0])
    for i, r in enumerate(starts):
        upcoming = front(starts[i + 1]) if i + 1 < len(starts) else None
        history = back(r, products, history)
        products = upcoming
    carry_ref[...] = history


def _attn_out_kernel(qstart_ref, kstart_ref, nfull_ref, nmask_ref,
                     qnt_ref, qrt_ref, kn_ref, kd_ref, vt_ref, ga_ref, pos_ref,
                     yb_ref, wo_ref, lng_ref, lnb_ref, x_hbm,
                     out_hbm,
                     m_ref, alpha_ref, acc_ref, bias_ref, xbuf_ref, obuf_ref, in_sem, out_sem,
                     *head_refs, npairs, nrows, res_scale):
    tq, tk = Q_TILE, K_TILE
    s_refs, smax_refs, p_refs = (head_refs[i * HEADS:(i + 1) * HEADS] for i in range(3))
    nq = ga_ref.shape[1] // tq
    b = pl.program_id(0)
    n_full = nfull_ref[b]
    n_all = n_full + nmask_ref[b]

    def x_copy():
        return pltpu.make_async_copy(x_hbm.at[b], xbuf_ref, in_sem.at[0])

    def out_copy():
        return pltpu.make_async_copy(obuf_ref, out_hbm.at[b], out_sem.at[0])

    x_copy().start()

    m_ref[...] = jnp.full(m_ref.shape, _MASKED, _f32)

    @pl.when(b == 0)
    def _():
        acc_ref[...] = jnp.zeros_like(acc_ref)

    def pair(i):
        qs = pl.multiple_of(qstart_ref[b * npairs + i], tq)
        ks = pl.multiple_of(kstart_ref[b * npairs + i], tk)
        return qs, ks

    chunks = [slice(c, c + S_CHUNK) for c in range(0, tk, S_CHUNK)]
    ones_rows = jnp.ones((L_ROWS, tk), _bf16)

    def make_bias(qs, ks):
        for c in range(0, tk, LANES):
            pk_row = pos_ref[0, :, pl.ds(ks + c, LANES)]
            pk_col = jnp.broadcast_to(pk_row, (LANES, LANES)).T
            for j in range(0, tq, LANES):
                keep = pk_col <= pos_ref[0, :, pl.ds(qs + j, LANES)]
                bias_ref[c:c + LANES, j:j + LANES] = jnp.where(keep, 0.0, _MASKED)

    def scores(h, qs, ks, masked):
        rows = slice(h * LANES, (h + 1) * LANES)
        qc = jnp.concatenate([qnt_ref[0, rows, pl.ds(qs, tq)],
                              qrt_ref[0, h * ROPE:(h + 1) * ROPE, pl.ds(qs, tq)],
                              jnp.zeros((ROPE, tq), _bf16)], axis=0)
        kc = jnp.concatenate([kn_ref[0, pl.ds(ks, tk), rows],
                              kd_ref[0, pl.ds(ks, tk), :]], axis=1)
        s = _dot(kc, qc)
        if masked:
            s = s + bias_ref[...]
        s_refs[h][...] = s
        mx = s[chunks[0], :]
        for c in chunks[1:]:
            mx = jnp.maximum(mx, s[c, :])
        smax_refs[h][...] = mx

    def softmax(h, qs):
        cols = pl.ds(qs, tq)
        sb = s_refs[h]
        m_old = m_ref[h, :, cols]
        m_new = jnp.maximum(m_old, jnp.max(smax_refs[h][...], axis=0, keepdims=True))
        alpha_ref[h] = jnp.exp2(m_old - m_new)
        m_ref[h, :, cols] = m_new
        for c in chunks:
            p_refs[h][c, :] = jnp.exp2(sb[c, :] - m_new).astype(_bf16)

    def pv(h, qs, ks):
        rows = slice(h * LANES, (h + 1) * LANES)
        cols = pl.ds(qs, tq)
        v_ext = jnp.concatenate([vt_ref[0, rows, pl.ds(ks, tk)], ones_rows], axis=0)
        acc_ref[h, :, cols] = alpha_ref[h] * acc_ref[h, :, cols] + _dot(v_ext, p_refs[h][...])

    def step(i, next_masked):
        qs, ks = pair(i)
        if next_masked is not None:
            qs_n, ks_n = pair(i + 1)
            if next_masked:
                make_bias(qs_n, ks_n)
        for h in range(HEADS):
            if next_masked is not None:
                scores(h, qs_n, ks_n, next_masked)
            pv(h, qs, ks)
            if h + 1 < HEADS:
                softmax(h + 1, qs)
            elif next_masked is not None:
                softmax(0, qs_n)

    qs0, ks0 = pair(0)

    @pl.when(n_full > 0)
    def _():
        for h in range(HEADS):
            scores(h, qs0, ks0, False)

    @pl.when(n_full == 0)
    def _():
        make_bias(qs0, ks0)
        for h in range(HEADS):
            scores(h, qs0, ks0, True)

    softmax(0, qs0)

    def step_next_full(i, carry):
        step(i, False)
        return carry

    def step_next_masked(i, carry):
        step(i, True)
        return carry

    first_masked_next = jnp.maximum(n_full - 1, 0)
    lax.fori_loop(0, first_masked_next, step_next_full, 0)
    lax.fori_loop(first_masked_next, n_all - 1, step_next_masked, 0)
    step(n_all - 1, None)

    x_copy().wait()

    @pl.when(b > 0)
    def _():
        out_copy().wait()

    for qt in range(nq):
        rows = slice(qt * tq, (qt + 1) * tq)
        ya = []
        for h in range(HEADS):
            cols = slice(h * LANES, (h + 1) * LANES)
            inv_l = 1.0 / acc_ref[h, V_DIM:V_DIM + 1, rows]
            o = (acc_ref[h, :V_DIM, rows] * inv_l).T
            ya.append((o * ga_ref[0, rows, cols].astype(_f32)).astype(_bf16))
        y = jnp.concatenate(ya + [yb_ref[0, rows, :]], axis=1)
        for r in range(0, tq, OUT_SUBTILE):
            sub = slice(qt * tq + r, qt * tq + r + OUT_SUBTILE)
            z = res_scale * xbuf_ref[sub, :] + _dot(y[r:r + OUT_SUBTILE], wo_ref[...])
            mu = jnp.mean(z, axis=-1, keepdims=True)
            zc = z - mu
            var = jnp.mean(zc * zc, axis=-1, keepdims=True)
            obuf_ref[sub, :] = zc * lax.rsqrt(var + LN_EPS) * lng_ref[...] + lnb_ref[...]
    out_copy().start()

    @pl.when(b == nrows - 1)
    def _():
        out_copy().wait()


def _rotate_half_cols(w):
    half = w.shape[-1] // 2
    return jnp.concatenate([-w[..., half:], w[..., :half]], axis=-1)


def _prepare_weights(w_in, w_uq, w_ukv, pool_w):
    w_in, w_uq, w_ukv, pool_w = (w.astype(_bf16) for w in (w_in, w_uq, w_ukv, pool_w))
    splits = np.cumsum([Q_RANK, KV_RANK, ROPE, MLA_WIDTH, POOL_WIDTH, POOL_WIDTH])[:-1]
    wq_l, wkv_l, wkr, wga, wu, wgb = jnp.split(w_in, [int(c) for c in splits], axis=1)
    w1 = jnp.concatenate([wq_l, wkv_l, wkr, _rotate_half_cols(wkr), wga, wu, wgb], axis=1)

    wq3 = w_uq.reshape(Q_RANK, HEADS, QK_DIM)
    wq_nope = wq3[:, :, :NOPE].reshape(Q_RANK, HEADS * NOPE)
    wq_rope = wq3[:, :, NOPE:].reshape(Q_RANK, HEADS * ROPE)
    wqt = jnp.concatenate([wq_nope, wq_rope], axis=1).T

    wkv3 = w_ukv.reshape(KV_RANK, HEADS, NOPE + V_DIM)
    wkn = wkv3[:, :, :NOPE].reshape(KV_RANK, HEADS * NOPE)
    wvt = wkv3[:, :, NOPE:].reshape(KV_RANK, HEADS * V_DIM).T

    z = jnp.zeros((POOL_GROUP, POOL_GROUP), pool_w.dtype)
    wp = jnp.stack([jnp.block([[pool_w[0], z], [z, pool_w[1]]]),
                    jnp.block([[pool_w[2], z], [z, pool_w[3]]])])
    return w1, wqt, wkn, wvt, wp


def _const_spec(shape):
    return pl.BlockSpec(shape, lambda *_: (0,) * len(shape), pipeline_mode=pl.Buffered(1))


def _layer(x, positions, w1, q_norm_g, wqt, kv_norm_g, wkn, wvt, wp, pool_scale, w_out,
           ln_g, ln_b):
    B, S, D = x.shape
    tm, tq, tk = TOKEN_TILE, Q_TILE, K_TILE
    ns, nq, nk = S // tm, S // tq, S // tk

    half = ROPE // 2
    inv_freq = ROPE_THETA ** (-jnp.arange(half, dtype=_f32) / half)
    freq_col = inv_freq.reshape(half, 1)
    pos_row = positions.reshape(B, 1, S)

    def tok_spec(width):
        return pl.BlockSpec((1, tm, width), lambda b, s: (b, s, 0))

    def feat_spec(width):
        return pl.BlockSpec((1, width, tm), lambda b, s: (b, 0, s))

    act = lambda width: jax.ShapeDtypeStruct((B, S, width), _bf16)
    act_t = lambda width: jax.ShapeDtypeStruct((B, width, S), _bf16)
    qnt, qrt, kn, kd, vt, ga, yb = pl.pallas_call(
        _proj_kernel,
        grid=(B, ns),
        in_specs=[tok_spec(D), feat_spec(1), _const_spec((half, 1)),
                  _const_spec(w1.shape), _const_spec((1, Q_RANK)), _const_spec(wqt.shape),
                  _const_spec((1, KV_RANK)), _const_spec(wkn.shape), _const_spec(wvt.shape),
                  _const_spec(wp.shape), _const_spec((1, POOL_WIDTH))],
        out_specs=[feat_spec(MLA_WIDTH), feat_spec(HEADS * ROPE), tok_spec(MLA_WIDTH),
                   tok_spec(LANES), feat_spec(MLA_WIDTH), tok_spec(MLA_WIDTH),
                   tok_spec(POOL_WIDTH)],
        out_shape=[act_t(MLA_WIDTH), act_t(HEADS * ROPE), act(MLA_WIDTH), act(LANES),
                   act_t(MLA_WIDTH), act(MLA_WIDTH), act(POOL_WIDTH)],
        scratch_shapes=[pltpu.VMEM((MAX_WINDOW, POOL_WIDTH), _f32)],
        compiler_params=pltpu.CompilerParams(
            dimension_semantics=("arbitrary", "arbitrary"),
            vmem_limit_bytes=VMEM_LIMIT_BYTES),
        name="mla_pool_proj",
    )(x, pos_row, freq_col, w1, q_norm_g.reshape(1, -1), wqt, kv_norm_g.reshape(1, -1), wkn,
      wvt, wp, pool_scale.reshape(1, -1))

    pq_blk = positions.reshape(B, nq, tq)
    pk_blk = positions.reshape(B, nk, tk)
    need = pq_blk.max(-1)[:, :, None] >= pk_blk.min(-1)[:, None, :]
    full = pq_blk.min(-1)[:, :, None] >= pk_blk.max(-1)[:, None, :]
    npairs = nq * nk
    rank = jnp.where(need & full, 0, jnp.where(need, 1, 2))
    rank = jnp.swapaxes(rank, 1, 2).reshape(B, npairs)
    order = jnp.argsort(rank, axis=1, stable=True).astype(jnp.int32)
    q_start = (order % nq) * tq
    k_start = (order // nq) * tk
    n_full = jnp.sum(rank == 0, axis=1, dtype=jnp.int32)
    n_mask = jnp.sum(rank == 1, axis=1, dtype=jnp.int32)

    row_spec = lambda d1, d2: pl.BlockSpec((1, d1, d2), lambda b, *_: (b, 0, 0))
    alpha = (2.0 * DEPTH) ** 0.25
    return pl.pallas_call(
        functools.partial(_attn_out_kernel, npairs=npairs, nrows=B, res_scale=alpha),
        grid_spec=pltpu.PrefetchScalarGridSpec(
            num_scalar_prefetch=4,
            grid=(B,),
            in_specs=[row_spec(MLA_WIDTH, S), row_spec(HEADS * ROPE, S), row_spec(S, MLA_WIDTH),
                      row_spec(S, LANES), row_spec(MLA_WIDTH, S), row_spec(S, MLA_WIDTH),
                      row_spec(1, S), row_spec(S, POOL_WIDTH),
                      _const_spec(w_out.shape), _const_spec((1, D)), _const_spec((1, D)),
                      pl.BlockSpec(memory_space=pl.ANY)],
            out_specs=pl.BlockSpec(memory_space=pl.ANY),
            scratch_shapes=[pltpu.VMEM((HEADS, 1, S), _f32),
                            pltpu.VMEM((HEADS, 1, tq), _f32),
                            pltpu.VMEM((HEADS, V_DIM + L_ROWS, S), _f32),
                            pltpu.VMEM((tk, tq), _f32),
                            pltpu.VMEM((S, D), _f32),
                            pltpu.VMEM((S, D), _f32),
                            pltpu.SemaphoreType.DMA((1,)),
                            pltpu.SemaphoreType.DMA((1,))]
            + [pltpu.VMEM((tk, tq), _f32) for _ in range(HEADS)]
            + [pltpu.VMEM((S_CHUNK, tq), _f32) for _ in range(HEADS)]
            + [pltpu.VMEM((tk, tq), _bf16) for _ in range(HEADS)]),
        out_shape=jax.ShapeDtypeStruct((B, S, D), x.dtype),
        compiler_params=pltpu.CompilerParams(
            dimension_semantics=("arbitrary",),
            vmem_limit_bytes=VMEM_LIMIT_BYTES),
        name="mla_attention_out",
    )(q_start.reshape(-1), k_start.reshape(-1), n_full, n_mask,
      qnt, qrt, kn, kd, vt, ga, pos_row, yb, w_out, ln_g.reshape(1, -1), ln_b.reshape(1, -1), x)


def kernel(x, positions, w_in, q_norm_g, w_uq, kv_norm_g, w_ukv, pool_w, pool_scale, w_out,
           ln_g, ln_b):
    w1, wqt, wkn, wvt, wp = _prepare_weights(w_in, w_uq, w_ukv, pool_w)
    w_out_b = w_out.astype(_bf16)
    for layer in range(DEPTH):
        x = _layer(x, positions, w1, q_norm_g, wqt, kv_norm_g, wkn, wvt, wp, pool_scale,
                   w_out_b, ln_g[layer], ln_b[layer])
    return x
```
